```python
import jax, jax.numpy as jnp
from jax import lax
import numpy as np

D_MODEL = 1024
BATCH = 8
SEQ = 4096
DEPTH = 1

PLE_DIM = 256
MLA_HEADS = 8
QK_NOPE_DIM = 64
QK_ROPE_DIM = 32
V_HEAD_DIM = 64
Q_LORA_RANK = 384
KV_LORA_RANK = 256
ROPE_THETA = 10000.0
Q_BLOCK = 128
CONV_CHANNELS = 512
CONV_WIDTH = 31
MLA_WIDTH = MLA_HEADS * V_HEAD_DIM
D_MIX = MLA_WIDTH + CONV_CHANNELS
IN_PROJ_DIM = Q_LORA_RANK + KV_LORA_RANK + QK_ROPE_DIM + 2 * CONV_CHANNELS
D_FF = 2816
FFN_CONV_WIDTH = 3
NORM_EPS = 1e-6

kernel_name = "hymba_mla_conformer_convffn_sandwich_ple"


def rms_norm(x, g):
    xf = x.astype(jnp.float32)
    y = xf * lax.rsqrt(jnp.mean(xf * xf, axis=-1, keepdims=True) + NORM_EPS)
    return (y * g.astype(jnp.float32)).astype(x.dtype)


def layer_norm(x, g, b):
    xf = x.astype(jnp.float32)
    mu = jnp.mean(xf, axis=-1, keepdims=True)
    xc = xf - mu
    y = xc * lax.rsqrt(jnp.mean(xc * xc, axis=-1, keepdims=True) + NORM_EPS)
    return (y * g.astype(jnp.float32) + b.astype(jnp.float32)).astype(x.dtype)


def causal_depthwise_conv(x, w, b):
    k = w.shape[0]
    out = lax.conv_general_dilated(
        x, w[:, None, :].astype(x.dtype), window_strides=(1,), padding=((k - 1, 0),),
        dimension_numbers=('NWC', 'WIO', 'NWC'), feature_group_count=x.shape[-1])
    return out + b.astype(x.dtype)


def apply_rope(x, cos, sin):
    x1, x2 = jnp.split(x, 2, axis=-1)
    return jnp.concatenate([x1 * cos - x2 * sin, x1 * sin + x2 * cos], axis=-1)


def causal_mla_attention(q_nope, q_rope, k_nope, k_rope, v):
    b, s, h, _ = q_nope.shape
    nb = s // Q_BLOCK
    scale = (QK_NOPE_DIM + QK_ROPE_DIM) ** -0.5
    k_idx = jnp.arange(s)

    def to_blocks(t):
        return jnp.moveaxis(t.reshape(b, nb, Q_BLOCK, *t.shape[2:]), 1, 0)

    def block(args):
        qn, qr, start = args
        sc = (jnp.einsum('bqhd,bkhd->bhqk', qn, k_nope, preferred_element_type=jnp.float32)
              + jnp.einsum('bqhr,bkr->bhqk', qr, k_rope, preferred_element_type=jnp.float32)) * scale
        q_idx = start + jnp.arange(Q_BLOCK)
        sc = jnp.where(k_idx[None, :] <= q_idx[:, None], sc, -jnp.inf)
        pr = jax.nn.softmax(sc, axis=-1).astype(v.dtype)
        return jnp.einsum('bhqk,bkhd->bqhd', pr, v)

    starts = jnp.arange(nb) * Q_BLOCK
    out = lax.map(block, (to_blocks(q_nope), to_blocks(q_rope), starts))
    return jnp.moveaxis(out, 0, 1).reshape(b, s, h * V_HEAD_DIM)


def hybrid_mixer(xn, cos, sin, w_in, g_q_a, w_q_b, g_kv_a, w_kv_b,
                 conv_w, conv_b, conv_ln_g, conv_ln_b, w_o):
    b, s, _ = xn.shape
    proj = xn @ w_in
    o1 = Q_LORA_RANK
    o2 = o1 + KV_LORA_RANK
    o3 = o2 + QK_ROPE_DIM
    q_a, kv_a, k_rope, conv_in = jnp.split(proj, [o1, o2, o3], axis=-1)
    q = (rms_norm(q_a, g_q_a) @ w_q_b).reshape(b, s, MLA_HEADS, QK_NOPE_DIM + QK_ROPE_DIM)
    q_nope, q_rope = jnp.split(q, [QK_NOPE_DIM], axis=-1)
    kv = (rms_norm(kv_a, g_kv_a) @ w_kv_b).reshape(b, s, MLA_HEADS, QK_NOPE_DIM + V_HEAD_DIM)
    k_nope, v = jnp.split(kv, [QK_NOPE_DIM], axis=-1)
    q_rope = apply_rope(q_rope, cos[:, :, None, :], sin[:, :, None, :])
    k_rope = apply_rope(k_rope, cos, sin)
    attn = causal_mla_attention(q_nope, q_rope, k_nope, k_rope, v)
    a, gate = jnp.split(conv_in, 2, axis=-1)
    c = a * jax.nn.sigmoid(gate)
    c = causal_depthwise_conv(c, conv_w, conv_b)
    c = jax.nn.silu(layer_norm(c, conv_ln_g, conv_ln_b))
    return jnp.concatenate([attn, c], axis=-1) @ w_o


def conv_gated_ffn(xn, w_gate, w_up, dw_w, dw_b, w_down):
    g = causal_depthwise_conv(xn @ w_gate, dw_w, dw_b)
    return (jax.nn.gelu(g, approximate=True) * (xn @ w_up)) @ w_down


def setup_inputs(seed: int = 0) -> dict:
    key = jax.random.key(seed)
    ks = iter(jax.random.split(key, 40))

    def w(shape, fan_in):
        return jax.random.normal(next(ks), shape, jnp.float32) * fan_in ** -0.5

    def gain(shape):
        return 1.0 + 0.05 * jax.random.normal(next(ks), shape, jnp.float32)

    def bias(shape):
        return 0.01 * jax.random.normal(next(ks), shape, jnp.float32)

    L = DEPTH
    x = jax.random.normal(next(ks), (BATCH, SEQ, D_MODEL), jnp.float32)
    p = jax.random.normal(next(ks), (DEPTH, BATCH, SEQ, PLE_DIM), jnp.float32)
    positions = jnp.broadcast_to(jnp.arange(SEQ, dtype=jnp.int32)[None, :], (BATCH, SEQ))
    return {
        "x": x,
        "p": p,
        "positions": positions,
        "g_mix_pre": gain((L, D_MODEL)),
        "w_in": w((L, D_MODEL, IN_PROJ_DIM), D_MODEL),
        "g_q_a": gain((L, Q_LORA_RANK)),
        "w_q_b": w((L, Q_LORA_RANK, MLA_HEADS * (QK_NOPE_DIM + QK_ROPE_DIM)), Q_LORA_RANK),
        "g_kv_a": gain((L, KV_LORA_RANK)),
        "w_kv_b": w((L, KV_LORA_RANK, MLA_HEADS * (QK_NOPE_DIM + V_HEAD_DIM)), KV_LORA_RANK),
        "conv_w": w((L, CONV_WIDTH, CONV_CHANNELS), CONV_WIDTH),
        "conv_b": bias((L, CONV_CHANNELS)),
        "conv_ln_g": gain((L, CONV_CHANNELS)),
        "conv_ln_b": bias((L, CONV_CHANNELS)),
        "w_o": w((L, D_MIX, D_MODEL), D_MIX),
        "g_mix_post": gain((L, D_MODEL)),
        "g_ffn_pre": gain((L, D_MODEL)),
        "w_ffn_gate": w((L, D_MODEL, D_FF), D_MODEL),
        "w_ffn_up": w((L, D_MODEL, D_FF), D_MODEL),
        "ffn_dw_w": w((L, FFN_CONV_WIDTH, D_FF), FFN_CONV_WIDTH),
        "ffn_dw_b": bias((L, D_FF)),
        "w_ffn_down": w((L, D_FF, D_MODEL), D_FF),
        "g_ffn_post": gain((L, D_MODEL)),
        "w_ple_proj": w((L, PLE_DIM, D_MODEL), PLE_DIM),
        "g_ple": gain((L, D_MODEL)),
        "w_ple_gate": w((L, D_MODEL, D_MODEL), D_MODEL),
    }


def reference(x, p, positions, g_mix_pre, w_in, g_q_a, w_q_b, g_kv_a, w_kv_b,
              conv_w, conv_b, conv_ln_g, conv_ln_b, w_o, g_mix_post, g_ffn_pre,
              w_ffn_gate, w_ffn_up, ffn_dw_w, ffn_dw_b, w_ffn_down, g_ffn_post,
              w_ple_proj, g_ple, w_ple_gate):
    inv_freq = ROPE_THETA ** (-jnp.arange(0, QK_ROPE_DIM, 2, dtype=jnp.float32) / QK_ROPE_DIM)
    ang = positions.astype(jnp.float32)[..., None] * inv_freq
    cos = jnp.cos(ang).astype(x.dtype)
    sin = jnp.sin(ang).astype(x.dtype)

    h = x
    for i in range(DEPTH):
        mix = hybrid_mixer(rms_norm(h, g_mix_pre[i]), cos, sin, w_in[i], g_q_a[i], w_q_b[i],
                           g_kv_a[i], w_kv_b[i], conv_w[i], conv_b[i], conv_ln_g[i],
                           conv_ln_b[i], w_o[i])
        h = h + rms_norm(mix, g_mix_post[i])
        ffn = conv_gated_ffn(rms_norm(h, g_ffn_pre[i]), w_ffn_gate[i], w_ffn_up[i],
                             ffn_dw_w[i], ffn_dw_b[i], w_ffn_down[i])
        h = h + rms_norm(ffn, g_ffn_post[i])
        e = rms_norm(p[i] @ w_ple_proj[i], g_ple[i])
        h = h + jax.nn.sigmoid(h @ w_ple_gate[i]) * e
    return h
```

```python
import functools

import jax
import jax.numpy as jnp
from jax import lax
from jax.experimental import pallas as pl
from jax.experimental.pallas import tpu as pltpu

D_MODEL = 1024
PLE_DIM = 256
MLA_HEADS = 8
QK_NOPE_DIM = 64
QK_ROPE_DIM = 32
QK_DIM = QK_NOPE_DIM + QK_ROPE_DIM
V_HEAD_DIM = 64
Q_LORA_RANK = 384
KV_LORA_RANK = 256
ROPE_THETA = 10000.0
CONV_CHANNELS = 512
CONV_WIDTH = 31
D_FF = 2816
FFN_CONV_WIDTH = 3
NORM_EPS = 1e-6

LANES = 128
HEAD_SLAB = LANES
HALF_ROPE = QK_ROPE_DIM // 2
CONV_HALO = 32
FFN_HALO = 8
SCORE_SCALE = (QK_DIM ** -0.5) * 1.4426950408889634

TM_IN = 512
TM_OUT = 256
VMEM_LIMIT = 56 * 1024 * 1024

BF16 = jnp.bfloat16
F32 = jnp.float32
NT_DIMS = (((1,), (1,)), ((), ()))


def _rms(x, g):
    return x * lax.rsqrt(jnp.mean(x * x, axis=-1, keepdims=True) + NORM_EPS) * g


def _const_spec(shape):
    nd = len(shape)
    return pl.BlockSpec(shape, lambda *_: (0,) * nd, pipeline_mode=pl.Buffered(1))


def _mixer_in_kernel(pos_ref, x_ref, g_pre_ref, w1_ref, gq_ref, wqt_ref, gkv_ref, wkv_ref, wvt_ref,
                     invf_ref, convw_ref, convb_ref, lng_ref, lnb_ref,
                     qt_ref, k_ref, vt_ref, c2_ref, cbuf_ref, *, tm):
    i = pl.program_id(1)
    x = x_ref[0]
    xn = _rms(x, g_pre_ref[...]).astype(BF16)
    proj = jnp.dot(xn, w1_ref[...], preferred_element_type=F32)
    o_kv = Q_LORA_RANK
    o_a = o_kv + KV_LORA_RANK
    o_g = o_a + CONV_CHANNELS
    o_kr = o_g + CONV_CHANNELS
    o_kp = o_kr + LANES
    qn = _rms(proj[:, :o_kv], gq_ref[...]).astype(BF16)
    kvn = _rms(proj[:, o_kv:o_a], gkv_ref[...]).astype(BF16)

    ang = invf_ref[...] * pos_ref[0].astype(F32)
    cos_t = jnp.cos(ang)
    sin_t = jnp.sin(ang)

    qt = lax.dot_general(wqt_ref[...], qn, NT_DIMS, preferred_element_type=F32) * SCORE_SCALE
    zero_rows = jnp.zeros((HEAD_SLAB - QK_DIM, tm), BF16)
    for h in range(MLA_HEADS):
        r0 = h * QK_DIM
        o0 = h * HEAD_SLAB
        x1 = qt[r0 + QK_NOPE_DIM:r0 + QK_NOPE_DIM + HALF_ROPE]
        x2 = qt[r0 + QK_NOPE_DIM + HALF_ROPE:r0 + QK_DIM]
        qt_ref[0, o0:o0 + QK_NOPE_DIM, :] = qt[r0:r0 + QK_NOPE_DIM].astype(BF16)
        qt_ref[0, o0 + QK_NOPE_DIM:o0 + QK_NOPE_DIM + HALF_ROPE, :] = (x1 * cos_t - x2 * sin_t).astype(BF16)
        qt_ref[0, o0 + QK_NOPE_DIM + HALF_ROPE:o0 + QK_DIM, :] = (x1 * sin_t + x2 * cos_t).astype(BF16)
        qt_ref[0, o0 + QK_DIM:o0 + HEAD_SLAB, :] = zero_rows

    pad_lo = jnp.zeros((QK_NOPE_DIM, tm), F32)
    pad_hi = jnp.zeros((HEAD_SLAB - QK_DIM, tm), F32)
    cos_tok = jnp.concatenate([pad_lo, cos_t, cos_t, pad_hi], axis=0).T
    sin_tok = jnp.concatenate([pad_lo, sin_t, sin_t, pad_hi], axis=0).T
    k_rope = proj[:, o_kr:o_kp] * cos_tok + proj[:, o_kp:o_kp + LANES] * sin_tok

    kfull = jnp.dot(kvn, wkv_ref[...], preferred_element_type=F32)
    lane = lax.broadcasted_iota(jnp.int32, (tm, HEAD_SLAB), 1)
    for h in range(MLA_HEADS):
        slab = kfull[:, h * HEAD_SLAB:(h + 1) * HEAD_SLAB]
        k_ref[0, :, h * HEAD_SLAB:(h + 1) * HEAD_SLAB] = jnp.where(lane < QK_NOPE_DIM, slab, k_rope).astype(BF16)

    vt = lax.dot_general(wvt_ref[...], kvn, NT_DIMS, preferred_element_type=F32)
    vt_ref[0, 0] = vt.astype(BF16)

    c = proj[:, o_a:o_g] * jax.nn.sigmoid(proj[:, o_g:o_kr])

    @pl.when(i == 0)
    def _():
        cbuf_ref[0:CONV_HALO, :] = jnp.zeros((CONV_HALO, CONV_CHANNELS), F32)

    cbuf_ref[CONV_HALO:CONV_HALO + tm, :] = c
    acc = jnp.broadcast_to(convb_ref[...], (tm, CONV_CHANNELS))
    base = CONV_HALO - (CONV_WIDTH - 1)
    for j in range(CONV_WIDTH):
        acc = acc + convw_ref[j:j + 1, :] * cbuf_ref[base + j:base + j + tm, :]
    cbuf_ref[0:CONV_HALO, :] = cbuf_ref[tm:tm + CONV_HALO, :]
    mu = jnp.mean(acc, axis=-1, keepdims=True)
    xc = acc - mu
    y = xc * lax.rsqrt(jnp.mean(xc * xc, axis=-1, keepdims=True) + NORM_EPS) * lng_ref[...] + lnb_ref[...]
    c2_ref[0] = (y * jax.nn.sigmoid(y)).astype(BF16)


def _mixer_in(pos3, x, g_pre, w1, gq, wqt, gkv, wkv, wvt, invf, convw, convb, lng, lnb):
    b, s, d = x.shape
    tm = TM_IN
    nt = s // tm
    n1 = w1.shape[1]
    kern = functools.partial(_mixer_in_kernel, tm=tm)
    return pl.pallas_call(
        kern,
        grid=(b, nt),
        in_specs=[
            pl.BlockSpec((1, 1, tm), lambda bi, i: (bi, 0, i)),
            pl.BlockSpec((1, tm, d), lambda bi, i: (bi, i, 0)),
            _const_spec((1, d)),
            _const_spec((d, n1)),
            _const_spec((1, Q_LORA_RANK)),
            _const_spec(wqt.shape),
            _const_spec((1, KV_LORA_RANK)),
            _const_spec(wkv.shape),
            _const_spec(wvt.shape),
            _const_spec(invf.shape),
            _const_spec(convw.shape),
            _const_spec((1, CONV_CHANNELS)),
            _const_spec((1, CONV_CHANNELS)),
            _const_spec((1, CONV_CHANNELS)),
        ],
        out_specs=[
            pl.BlockSpec((1, MLA_HEADS * HEAD_SLAB, tm), lambda bi, i: (bi, 0, i)),
            pl.BlockSpec((1, tm, MLA_HEADS * HEAD_SLAB), lambda bi, i: (bi, i, 0)),
            pl.BlockSpec((1, 1, MLA_HEADS * V_HEAD_DIM, tm), lambda bi, i: (bi, i, 0, 0)),
            pl.BlockSpec((1, tm, CONV_CHANNELS), lambda bi, i: (bi, i, 0)),
        ],
        out_shape=[
            jax.ShapeDtypeStruct((b, MLA_HEADS * HEAD_SLAB, s), BF16),
            jax.ShapeDtypeStruct((b, s, MLA_HEADS * HEAD_SLAB), BF16),
            jax.ShapeDtypeStruct((b, nt, MLA_HEADS * V_HEAD_DIM, tm), BF16),
            jax.ShapeDtypeStruct((b, s, CONV_CHANNELS), BF16),
        ],
        scratch_shapes=[pltpu.VMEM((CONV_HALO + tm, CONV_CHANNELS), F32)],
        compiler_params=pltpu.CompilerParams(
            dimension_semantics=("arbitrary", "arbitrary"), vmem_limit_bytes=VMEM_LIMIT),
        name="mixer_in",
    )(pos3, x, g_pre, w1, gq, wqt, gkv, wkv, wvt, invf, convw, convb, lng, lnb)


HEADS_PER_STEP = 2


def _attn_kernel(qt_ref, k_ref, vt_ref, o_ref, *, t):
    qi = pl.program_id(2)
    row = lax.broadcasted_iota(jnp.int32, (t, t), 0)
    col = lax.broadcasted_iota(jnp.int32, (t, t), 1)
    outs = []
    for e in range(HEADS_PER_STEP):
        qte = qt_ref[0, e * HEAD_SLAB:(e + 1) * HEAD_SLAB, :]

        def tile(j, carry, masked, e=e, qte=qte):
            m, l, acc = carry
            kj = k_ref[0, pl.ds(pl.multiple_of(j * t, t), t), e * HEAD_SLAB:(e + 1) * HEAD_SLAB]
            s = jnp.dot(kj, qte, preferred_element_type=F32)
            if masked:
                s = jnp.where(row <= col, s, -jnp.inf)
            m_new = jnp.maximum(m, jnp.max(s, axis=0, keepdims=True))
            alpha = jnp.exp2(m - m_new)
            p = jnp.exp2(s - m_new)
            l = alpha * l + jnp.sum(p, axis=0, keepdims=True)
            vj = vt_ref[0, j, e * V_HEAD_DIM:(e + 1) * V_HEAD_DIM, :]
            acc = alpha * acc + jnp.dot(vj, p.astype(BF16), preferred_element_type=F32)
            return m_new, l, acc

        init = (jnp.full((1, t), -jnp.inf, F32), jnp.zeros((1, t), F32), jnp.zeros((V_HEAD_DIM, t), F32))
        carry = lax.fori_loop(0, qi, functools.partial(tile, masked=False), init)
        _, l, acc = tile(qi, carry, True)
        outs.append(acc * (1.0 / l))
    o_ref[0] = jnp.concatenate(outs, axis=0).T.astype(BF16)


def _mla_attn(qt, k, vt):
    b, _, s = qt.shape
    t = TM_IN
    nt = s // t
    pairs = MLA_HEADS // HEADS_PER_STEP
    return pl.pallas_call(
        functools.partial(_attn_kernel, t=t),
        grid=(b, pairs, nt),
        in_specs=[
            pl.BlockSpec((1, HEADS_PER_STEP * HEAD_SLAB, t), lambda bi, p, qi: (bi, p, qi)),
            pl.BlockSpec((1, s, HEADS_PER_STEP * HEAD_SLAB), lambda bi, p, qi: (bi, 0, p)),
            pl.BlockSpec((1, nt, HEADS_PER_STEP * V_HEAD_DIM, t), lambda bi, p, qi: (bi, 0, p, 0)),
        ],
        out_specs=pl.BlockSpec((1, t, HEADS_PER_STEP * V_HEAD_DIM), lambda bi, p, qi: (bi, qi, p)),
        out_shape=jax.ShapeDtypeStruct((b, s, MLA_HEADS * V_HEAD_DIM), BF16),
        compiler_params=pltpu.CompilerParams(
            dimension_semantics=("arbitrary", "arbitrary", "arbitrary"), vmem_limit_bytes=VMEM_LIMIT),
        name="mla_attn",
    )(qt, k, vt)


def _gelu_tanh(x):
    return x * (0.5 * (1.0 + jnp.tanh(0.7978845608028654 * (x + 0.044715 * (x * x * x)))))


def _mixer_out_kernel(x_ref, attn_ref, c2_ref, p_ref, wo_ref, g_mpost_ref, g_fpre_ref, wg_ref, wu_ref,
                      dww_ref, dwb_ref, wd_ref, g_fpost_ref, wpp_ref, g_ple_ref, wpg_ref,
                      o_ref, gbuf_ref, *, tm):
    i = pl.program_id(1)
    mix_in = jnp.concatenate([attn_ref[0], c2_ref[0]], axis=-1)
    mix = jnp.dot(mix_in, wo_ref[...], preferred_element_type=F32)
    h1 = x_ref[0] + _rms(mix, g_mpost_ref[...])
    hn = _rms(h1, g_fpre_ref[...]).astype(BF16)
    gate = jnp.dot(hn, wg_ref[...], preferred_element_type=F32)
    up = jnp.dot(hn, wu_ref[...], preferred_element_type=F32)

    @pl.when(i == 0)
    def _():
        gbuf_ref[0:FFN_HALO, :] = jnp.zeros((FFN_HALO, D_FF), F32)

    gbuf_ref[FFN_HALO:FFN_HALO + tm, :] = gate
    g = (dwb_ref[...] + dww_ref[0:1, :] * gbuf_ref[FFN_HALO - 2:FFN_HALO - 2 + tm, :]
         + dww_ref[1:2, :] * gbuf_ref[FFN_HALO - 1:FFN_HALO - 1 + tm, :] + dww_ref[2:3, :] * gate)
    gbuf_ref[0:FFN_HALO, :] = gbuf_ref[tm:tm + FFN_HALO, :]
    act = (_gelu_tanh(g) * up).astype(BF16)
    ffn = jnp.dot(act, wd_ref[...], preferred_element_type=F32)
    h2 = h1 + _rms(ffn, g_fpost_ref[...])
    e = _rms(jnp.dot(p_ref[0].astype(BF16), wpp_ref[...], preferred_element_type=F32), g_ple_ref[...])
    gt = jax.nn.sigmoid(jnp.dot(h2.astype(BF16), wpg_ref[...], preferred_element_type=F32))
    o_ref[0] = h2 + gt * e


def _mixer_out(x, attn, c2, p, wo, g_mpost, g_fpre, wg, wu, dww, dwb, wd, g_fpost, wpp, g_ple, wpg):
    b, s, d = x.shape
    tm = TM_OUT
    nt = s // tm
    tok = lambda w: pl.BlockSpec((1, tm, w), lambda bi, i: (bi, i, 0))
    return pl.pallas_call(
        functools.partial(_mixer_out_kernel, tm=tm),
        grid=(b, nt),
        in_specs=[
            tok(d), tok(MLA_HEADS * V_HEAD_DIM), tok(CONV_CHANNELS), tok(PLE_DIM),
            _const_spec(wo.shape), _const_spec((1, d)), _const_spec((1, d)),
            _const_spec(wg.shape), _const_spec(wu.shape), _const_spec(dww.shape), _const_spec((1, D_FF)),
            _const_spec(wd.shape), _const_spec((1, d)), _const_spec(wpp.shape), _const_spec((1, d)),
            _const_spec(wpg.shape),
        ],
        out_specs=tok(d),
        out_shape=jax.ShapeDtypeStruct((b, s, d), F32),
        scratch_shapes=[pltpu.VMEM((FFN_HALO + tm, D_FF), F32)],
        compiler_params=pltpu.CompilerParams(
            dimension_semantics=("arbitrary", "arbitrary"), vmem_limit_bytes=VMEM_LIMIT),
        name="mixer_out",
    )(x, attn, c2, p, wo, g_mpost, g_fpre, wg, wu, dww, dwb, wd, g_fpost, wpp, g_ple, wpg)


def kernel(x, p, positions, g_mix_pre, w_in, g_q_a, w_q_b, g_kv_a, w_kv_b, conv_w, conv_b, conv_ln_g, conv_ln_b, w_o, g_mix_post, g_ffn_pre, w_ffn_gate, w_ffn_up, ffn_dw_w, ffn_dw_b, w_ffn_down, g_ffn_post, w_ple_proj, g_ple, w_ple_gate):
    b, s, d = x.shape
    assert w_in.shape[0] == 1 and (b, s, d) == (x.shape[0], x.shape[1], D_MODEL)
    assert s % TM_IN == 0 and s % TM_OUT == 0
    row = lambda v: v[0][None, :].astype(F32)

    w_in0 = w_in[0]
    o_kr = Q_LORA_RANK + KV_LORA_RANK
    o_conv = o_kr + QK_ROPE_DIM
    w_kr = w_in0[:, o_kr:o_conv]
    w_kp = jnp.concatenate([-w_kr[:, HALF_ROPE:], w_kr[:, :HALF_ROPE]], axis=1)
    zc = lambda n: jnp.zeros((d, n), F32)
    slab = lambda w: jnp.concatenate([zc(QK_NOPE_DIM), w, zc(HEAD_SLAB - QK_DIM)], axis=1)
    w1 = jnp.concatenate([w_in0[:, :o_kr], w_in0[:, o_conv:], slab(w_kr), slab(w_kp)], axis=1).astype(BF16)

    wqt = w_q_b[0].T.astype(BF16)
    wkv = w_kv_b[0].astype(BF16)
    wvt = (w_kv_b[0].reshape(KV_LORA_RANK, MLA_HEADS, QK_NOPE_DIM + V_HEAD_DIM)[:, :, QK_NOPE_DIM:]
           .reshape(KV_LORA_RANK, MLA_HEADS * V_HEAD_DIM).T.astype(BF16))
    invf = (ROPE_THETA ** (-jnp.arange(0, QK_ROPE_DIM, 2, dtype=F32) / QK_ROPE_DIM))[:, None]
    pos3 = positions.reshape(b, 1, s)

    qt, k, vt, c2 = _mixer_in(pos3, x, row(g_mix_pre), w1, row(g_q_a), wqt, row(g_kv_a), wkv, wvt, invf,
                              conv_w[0].astype(F32), row(conv_b), row(conv_ln_g), row(conv_ln_b))
    attn = _mla_attn(qt, k, vt)
    return _mixer_out(x, attn, c2, p[0], w_o[0].astype(BF16), row(g_mix_post), row(g_ffn_pre),
                      w_ffn_gate[0].astype(BF16), w_ffn_up[0].astype(BF16), ffn_dw_w[0].astype(F32),
                      row(ffn_dw_b), w_ffn_down[0].astype(BF16), row(g_ffn_post),
                      w_ple_proj[0].astype(BF16), row(g_ple), w_ple_gate[0].astype(BF16))
```

```python
import functools

import jax
import jax.numpy as jnp
from jax import lax
from jax.experimental import pallas as pl
from jax.experimental.pallas import tpu as pltpu

D_MODEL = 1024
PLE_DIM = 256
MLA_HEADS = 8
QK_NOPE_DIM = 64
QK_ROPE_DIM = 32
QK_DIM = QK_NOPE_DIM + QK_ROPE_DIM
V_HEAD_DIM = 64
Q_LORA_RANK = 384
KV_LORA_RANK = 256
ROPE_THETA = 10000.0
CONV_CHANNELS = 512
CONV_WIDTH = 31
D_FF = 2816
FFN_CONV_WIDTH = 3
NORM_EPS = 1e-6

LANES = 128
SUBLANES = 8
FFN_CHUNK = 256
HEAD_SLAB = LANES
HALF_ROPE = QK_ROPE_DIM // 2
CONV_HALO = 32
FFN_HALO = 8
SCORE_SCALE = (QK_DIM ** -0.5) * 1.4426950408889634

TM_IN = 512
SUB_IN = 256
TM_OUT = 256
VMEM_LIMIT = 56 * 1024 * 1024

BF16 = jnp.bfloat16
F32 = jnp.float32
NT_DIMS = (((1,), (1,)), ((), ()))


def _rms(x, g):
    return x * lax.rsqrt(jnp.mean(x * x, axis=-1, keepdims=True) + NORM_EPS) * g


def _const_spec(shape):
    nd = len(shape)
    return pl.BlockSpec(shape, lambda *_: (0,) * nd, pipeline_mode=pl.Buffered(1))


def _conformer_conv(c, h, sub, convw_ref, convb_ref, lng_ref, lnb_ref, cext_ref, shift_ref):
    lo = h * sub
    cext_ref[CONV_HALO + lo:CONV_HALO + lo + sub, :] = c
    span = sub + CONV_HALO - SUBLANES
    for r in range(1, SUBLANES):
        shift_ref[r - 1, h, 0:span, :] = cext_ref[lo + r:lo + r + span, :]
    acc = jnp.broadcast_to(convb_ref[...], (sub, CONV_CHANNELS))
    for j in range(CONV_WIDTH):
        a, r = divmod(CONV_HALO - (CONV_WIDTH - 1) + j, SUBLANES)
        if r == 0:
            tap = cext_ref[lo + a * SUBLANES:lo + a * SUBLANES + sub, :]
        else:
            tap = shift_ref[r - 1, h, a * SUBLANES:a * SUBLANES + sub, :]
        acc = acc + convw_ref[j:j + 1, :] * tap
    mu = jnp.mean(acc, axis=-1, keepdims=True)
    xc = acc - mu
    y = xc * lax.rsqrt(jnp.mean(xc * xc, axis=-1, keepdims=True) + NORM_EPS) * lng_ref[...] + lnb_ref[...]
    return (y * jax.nn.sigmoid(y)).astype(BF16)


def _mixer_in_kernel(pos_ref, x_ref, g_pre_ref, w1_ref, gq_ref, wqt_ref, gkv_ref, wkv_ref, wvt_ref,
                     invf_ref, convw_ref, convb_ref, lng_ref, lnb_ref,
                     qt_ref, k_ref, vt_ref, c2_ref, cext_ref, shift_ref, *, tm, sub):
    o_kv = Q_LORA_RANK
    o_a = o_kv + KV_LORA_RANK
    o_g = o_a + CONV_CHANNELS
    o_kr = o_g + CONV_CHANNELS
    o_kp = o_kr + LANES

    @pl.when(pl.program_id(1) == 0)
    def _():
        cext_ref[0:CONV_HALO, :] = jnp.zeros((CONV_HALO, CONV_CHANNELS), F32)

    for h in range(tm // sub):
        rows = slice(h * sub, (h + 1) * sub)
        xn = _rms(x_ref[0, rows, :], g_pre_ref[...]).astype(BF16)
        proj = jnp.dot(xn, w1_ref[...], preferred_element_type=F32)

        c = proj[:, o_a:o_g] * jax.nn.sigmoid(proj[:, o_g:o_kr])
        c2_ref[0, rows, :] = _conformer_conv(c, h, sub, convw_ref, convb_ref, lng_ref, lnb_ref,
                                             cext_ref, shift_ref)

        qn = _rms(proj[:, :o_kv], gq_ref[...]).astype(BF16)
        kvn = _rms(proj[:, o_kv:o_a], gkv_ref[...]).astype(BF16)

        ang = invf_ref[...] * pos_ref[0, :, rows].astype(F32)
        cos_t = jnp.cos(ang)
        sin_t = jnp.sin(ang)

        qt = lax.dot_general(wqt_ref[...], qn, NT_DIMS, preferred_element_type=F32) * SCORE_SCALE
        zero_rows = jnp.zeros((HEAD_SLAB - QK_DIM, sub), BF16)
        for hd in range(MLA_HEADS):
            r0 = hd * QK_DIM
            o0 = hd * HEAD_SLAB
            x1 = qt[r0 + QK_NOPE_DIM:r0 + QK_NOPE_DIM + HALF_ROPE]
            x2 = qt[r0 + QK_NOPE_DIM + HALF_ROPE:r0 + QK_DIM]
            qt_ref[0, o0:o0 + QK_NOPE_DIM, rows] = qt[r0:r0 + QK_NOPE_DIM].astype(BF16)
            qt_ref[0, o0 + QK_NOPE_DIM:o0 + QK_NOPE_DIM + HALF_ROPE, rows] = (x1 * cos_t - x2 * sin_t).astype(BF16)
            qt_ref[0, o0 + QK_NOPE_DIM + HALF_ROPE:o0 + QK_DIM, rows] = (x1 * sin_t + x2 * cos_t).astype(BF16)
            qt_ref[0, o0 + QK_DIM:o0 + HEAD_SLAB, rows] = zero_rows

        pad_lo = jnp.zeros((QK_NOPE_DIM, sub), F32)
        pad_hi = jnp.zeros((HEAD_SLAB - QK_DIM, sub), F32)
        cos_tok = jnp.concatenate([pad_lo, cos_t, cos_t, pad_hi], axis=0).T
        sin_tok = jnp.concatenate([pad_lo, sin_t, sin_t, pad_hi], axis=0).T
        k_rope = proj[:, o_kr:o_kp] * cos_tok + proj[:, o_kp:o_kp + LANES] * sin_tok

        kfull = jnp.dot(kvn, wkv_ref[...], preferred_element_type=F32)
        lane = lax.broadcasted_iota(jnp.int32, (sub, HEAD_SLAB), 1)
        for hd in range(MLA_HEADS):
            slab = kfull[:, hd * HEAD_SLAB:(hd + 1) * HEAD_SLAB]
            k_ref[0, rows, hd * HEAD_SLAB:(hd + 1) * HEAD_SLAB] = (
                jnp.where(lane < QK_NOPE_DIM, slab, k_rope).astype(BF16))

        vt = lax.dot_general(wvt_ref[...], kvn, NT_DIMS, preferred_element_type=F32)
        vt_ref[0, 0, :, rows] = vt.astype(BF16)

    cext_ref[0:CONV_HALO, :] = cext_ref[tm:tm + CONV_HALO, :]


def _mixer_in(pos3, x, g_pre, w1, gq, wqt, gkv, wkv, wvt, invf, convw, convb, lng, lnb):
    b, s, d = x.shape
    tm, sub = TM_IN, SUB_IN
    nt = s // tm
    n1 = w1.shape[1]
    chan = _const_spec((1, CONV_CHANNELS))
    return pl.pallas_call(
        functools.partial(_mixer_in_kernel, tm=tm, sub=sub),
        grid=(b, nt),
        in_specs=[
            pl.BlockSpec((1, 1, tm), lambda bi, i: (bi, 0, i)),
            pl.BlockSpec((1, tm, d), lambda bi, i: (bi, i, 0)),
            _const_spec((1, d)),
            _const_spec((d, n1)),
            _const_spec((1, Q_LORA_RANK)),
            _const_spec(wqt.shape),
            _const_spec((1, KV_LORA_RANK)),
            _const_spec(wkv.shape),
            _const_spec(wvt.shape),
            _const_spec(invf.shape),
            _const_spec(convw.shape), chan, chan, chan,
        ],
        out_specs=[
            pl.BlockSpec((1, MLA_HEADS * HEAD_SLAB, tm), lambda bi, i: (bi, 0, i)),
            pl.BlockSpec((1, tm, MLA_HEADS * HEAD_SLAB), lambda bi, i: (bi, i, 0)),
            pl.BlockSpec((1, 1, MLA_HEADS * V_HEAD_DIM, tm), lambda bi, i: (bi, i, 0, 0)),
            pl.BlockSpec((1, tm, CONV_CHANNELS), lambda bi, i: (bi, i, 0)),
        ],
        out_shape=[
            jax.ShapeDtypeStruct((b, MLA_HEADS * HEAD_SLAB, s), BF16),
            jax.ShapeDtypeStruct((b, s, MLA_HEADS * HEAD_SLAB), BF16),
            jax.ShapeDtypeStruct((b, nt, MLA_HEADS * V_HEAD_DIM, tm), BF16),
            jax.ShapeDtypeStruct((b, s, CONV_CHANNELS), BF16),
        ],
        scratch_shapes=[pltpu.VMEM((CONV_HALO + tm, CONV_CHANNELS), F32),
                        pltpu.VMEM((SUBLANES - 1, tm // sub, CONV_HALO + sub, CONV_CHANNELS), F32)],
        compiler_params=pltpu.CompilerParams(
            dimension_semantics=("arbitrary", "arbitrary"), vmem_limit_bytes=VMEM_LIMIT),
        name="mixer_in",
    )(pos3, x, g_pre, w1, gq, wqt, gkv, wkv, wvt, invf, convw, convb, lng, lnb)


HEADS_PER_STEP = 4
L_ROWS = 16


def _attn_kernel(qt_ref, k_ref, vt_ref, o_ref, s_scr, cm_scr, *, t):
    qi = pl.program_id(2)
    ones = jnp.ones((L_ROWS, t), BF16)
    lead_heads = range(0, HEADS_PER_STEP, 2)
    lag_heads = range(1, HEADS_PER_STEP, 2)

    def scores(e, j, masked):
        kj = k_ref[0, pl.ds(pl.multiple_of(j * t, t), t), e * HEAD_SLAB:(e + 1) * HEAD_SLAB]
        s = jnp.dot(kj, qt_ref[0, e * HEAD_SLAB:(e + 1) * HEAD_SLAB, :], preferred_element_type=F32)
        if masked:
            row = lax.broadcasted_iota(jnp.int32, (t, t), 0)
            col = lax.broadcasted_iota(jnp.int32, (t, t), 1)
            s = jnp.where(row <= col, s, -jnp.inf)
        return s, jnp.max(s, axis=0, keepdims=True)

    def update(e, j, s, cm, state):
        m, acc = state
        m_new = jnp.maximum(m, cm)
        alpha = jnp.exp2(m - m_new)
        p = jnp.exp2(s - m_new).astype(BF16)
        vj = jnp.concatenate([vt_ref[0, j, e * V_HEAD_DIM:(e + 1) * V_HEAD_DIM, :], ones], axis=0)
        return m_new, alpha * acc + jnp.dot(vj, p, preferred_element_type=F32)

    def park(e, s, cm):
        s_scr[e // 2] = s
        cm_scr[e // 2] = cm

    state = {e: (jnp.full((1, t), -jnp.inf, F32), jnp.zeros((V_HEAD_DIM + L_ROWS, t), F32))
             for e in range(HEADS_PER_STEP)}
    for e in lead_heads:
        s, cm = scores(e, qi, True)
        state[e] = update(e, qi, s, cm, state[e])
    for e in lag_heads:
        park(e, *scores(e, qi, True))

    def body(j, carry):
        st = dict(zip(range(HEADS_PER_STEP), carry))
        lead_sc = {e: scores(e, j, False) for e in lead_heads}
        j_prev = jnp.where(j == 0, qi, j - 1)
        for e in lag_heads:
            st[e] = update(e, j_prev, s_scr[e // 2], cm_scr[e // 2], st[e])
        for e in lag_heads:
            park(e, *scores(e, j, False))
        for e in lead_heads:
            st[e] = update(e, j, *lead_sc[e], st[e])
        return tuple(st[e] for e in range(HEADS_PER_STEP))

    carry = lax.fori_loop(0, qi, body, tuple(state[e] for e in range(HEADS_PER_STEP)))
    state = dict(zip(range(HEADS_PER_STEP), carry))
    j_last = jnp.maximum(qi - 1, 0)
    for e in lag_heads:
        state[e] = update(e, j_last, s_scr[e // 2], cm_scr[e // 2], state[e])
    outs = [state[e][1][:V_HEAD_DIM] * (1.0 / state[e][1][V_HEAD_DIM:V_HEAD_DIM + 1])
            for e in range(HEADS_PER_STEP)]
    o_ref[0] = jnp.concatenate(outs, axis=0).T.astype(BF16)


def _mla_attn(qt, k, vt):
    b, _, s = qt.shape
    t = TM_IN
    nt = s // t
    groups = MLA_HEADS // HEADS_PER_STEP
    return pl.pallas_call(
        functools.partial(_attn_kernel, t=t),
        grid=(b, groups, nt),
        in_specs=[
            pl.BlockSpec((1, HEADS_PER_STEP * HEAD_SLAB, t), lambda bi, p, qi: (bi, p, qi)),
            pl.BlockSpec((1, s, HEADS_PER_STEP * HEAD_SLAB), lambda bi, p, qi: (bi, 0, p)),
            pl.BlockSpec((1, nt, HEADS_PER_STEP * V_HEAD_DIM, t), lambda bi, p, qi: (bi, 0, p, 0)),
        ],
        out_specs=pl.BlockSpec((1, t, HEADS_PER_STEP * V_HEAD_DIM), lambda bi, p, qi: (bi, qi, p)),
        out_shape=jax.ShapeDtypeStruct((b, s, MLA_HEADS * V_HEAD_DIM), BF16),
        scratch_shapes=[pltpu.VMEM((HEADS_PER_STEP // 2, t, t), F32),
                        pltpu.VMEM((HEADS_PER_STEP // 2, 1, t), F32)],
        compiler_params=pltpu.CompilerParams(
            dimension_semantics=("arbitrary", "arbitrary", "arbitrary"), vmem_limit_bytes=VMEM_LIMIT),
        name="mla_attn",
    )(qt, k, vt)


def _gelu_tanh(x):
    return x * (0.5 * (1.0 + jnp.tanh(0.7978845608028654 * (x + 0.044715 * (x * x * x)))))


def _mixer_out_kernel(x_ref, attn_ref, c2_ref, p_ref, wo_ref, g_mpost_ref, g_fpre_ref, wg_ref, wu_ref,
                      dww_ref, dwb_ref, wd_ref, g_fpost_ref, wpp_ref, g_ple_ref, wpg_ref,
                      o_ref, gcarry_ref, *, tm):
    mix_in = jnp.concatenate([attn_ref[0], c2_ref[0]], axis=-1)
    mix = jnp.dot(mix_in, wo_ref[...], preferred_element_type=F32)
    ple = jnp.dot(p_ref[0].astype(BF16), wpp_ref[...], preferred_element_type=F32)
    h1 = x_ref[0] + _rms(mix, g_mpost_ref[...])
    hn = _rms(h1, g_fpre_ref[...]).astype(BF16)

    @pl.when(pl.program_id(1) == 0)
    def _():
        gcarry_ref[...] = jnp.zeros((FFN_HALO, D_FF), F32)

    def gate_up(c0):
        cols = slice(c0, c0 + FFN_CHUNK)
        return (jnp.dot(hn, wg_ref[:, cols], preferred_element_type=F32),
                jnp.dot(hn, wu_ref[:, cols], preferred_element_type=F32))

    ffn = jnp.zeros((tm, D_MODEL), F32)
    nxt = gate_up(0)
    for c0 in range(0, D_FF, FFN_CHUNK):
        cols = slice(c0, c0 + FFN_CHUNK)
        gate, up = nxt
        if c0 + FFN_CHUNK < D_FF:
            nxt = gate_up(c0 + FFN_CHUNK)
        gext = jnp.concatenate([gcarry_ref[:, cols], gate], axis=0)
        g = (dwb_ref[:, cols] + dww_ref[0:1, cols] * gext[FFN_HALO - 2:FFN_HALO - 2 + tm]
             + dww_ref[1:2, cols] * gext[FFN_HALO - 1:FFN_HALO - 1 + tm] + dww_ref[2:3, cols] * gate)
        gcarry_ref[:, cols] = gate[tm - FFN_HALO:]
        act = (_gelu_tanh(g) * up).astype(BF16)
        ffn = ffn + jnp.dot(act, wd_ref[cols, :], preferred_element_type=F32)
    h2 = h1 + _rms(ffn, g_fpost_ref[...])
    gt = jax.nn.sigmoid(jnp.dot(h2.astype(BF16), wpg_ref[...], preferred_element_type=F32))
    o_ref[0] = h2 + gt * _rms(ple, g_ple_ref[...])


def _mixer_out(x, attn, c2, p, wo, g_mpost, g_fpre, wg, wu, dww, dwb, wd, g_fpost, wpp, g_ple, wpg):
    b, s, d = x.shape
    tm = TM_OUT
    nt = s // tm
    tok = lambda w: pl.BlockSpec((1, tm, w), lambda bi, i: (bi, i, 0))
    return pl.pallas_call(
        functools.partial(_mixer_out_kernel, tm=tm),
        grid=(b, nt),
        in_specs=[
            tok(d), tok(MLA_HEADS * V_HEAD_DIM), tok(CONV_CHANNELS), tok(PLE_DIM),
            _const_spec(wo.shape), _const_spec((1, d)), _const_spec((1, d)),
            _const_spec(wg.shape), _const_spec(wu.shape), _const_spec(dww.shape), _const_spec((1, D_FF)),
            _const_spec(wd.shape), _const_spec((1, d)), _const_spec(wpp.shape), _const_spec((1, d)),
            _const_spec(wpg.shape),
        ],
        out_specs=tok(d),
        out_shape=jax.ShapeDtypeStruct((b, s, d), F32),
        scratch_shapes=[pltpu.VMEM((FFN_HALO, D_FF), F32)],
        compiler_params=pltpu.CompilerParams(
            dimension_semantics=("arbitrary", "arbitrary"), vmem_limit_bytes=VMEM_LIMIT),
        name="mixer_out",
    )(x, attn, c2, p, wo, g_mpost, g_fpre, wg, wu, dww, dwb, wd, g_fpost, wpp, g_ple, wpg)


def kernel(x, p, positions, g_mix_pre, w_in, g_q_a, w_q_b, g_kv_a, w_kv_b, conv_w, conv_b, conv_ln_g, conv_ln_b, w_o, g_mix_post, g_ffn_pre, w_ffn_gate, w_ffn_up, ffn_dw_w, ffn_dw_b, w_ffn_down, g_ffn_post, w_ple_proj, g_ple, w_ple_gate):
    b, s, d = x.shape
    assert w_in.shape[0] == 1 and (b, s, d) == (x.shape[0], x.shape[1], D_MODEL)
    assert s % TM_IN == 0 and s % TM_OUT == 0
    row = lambda v: v[0][None, :].astype(F32)

    w_in0 = w_in[0]
    o_kr = Q_LORA_RANK + KV_LORA_RANK
    o_conv = o_kr + QK_ROPE_DIM
    w_kr = w_in0[:, o_kr:o_conv]
    w_kp = jnp.concatenate([-w_kr[:, HALF_ROPE:], w_kr[:, :HALF_ROPE]], axis=1)
    zc = lambda n: jnp.zeros((d, n), F32)
    slab = lambda w: jnp.concatenate([zc(QK_NOPE_DIM), w, zc(HEAD_SLAB - QK_DIM)], axis=1)
    w1 = jnp.concatenate([w_in0[:, :o_kr], w_in0[:, o_conv:], slab(w_kr), slab(w_kp)], axis=1).astype(BF16)

    wqt = w_q_b[0].T.astype(BF16)
    wkv = w_kv_b[0].astype(BF16)
    wvt = (w_kv_b[0].reshape(KV_LORA_RANK, MLA_HEADS, QK_NOPE_DIM + V_HEAD_DIM)[:, :, QK_NOPE_DIM:]
           .reshape(KV_LORA_RANK, MLA_HEADS * V_HEAD_DIM).T.astype(BF16))
    invf = (ROPE_THETA ** (-jnp.arange(0, QK_ROPE_DIM, 2, dtype=F32) / QK_ROPE_DIM))[:, None]
    pos3 = positions.reshape(b, 1, s)

    qt, k, vt, c2 = _mixer_in(pos3, x, row(g_mix_pre), w1, row(g_q_a), wqt, row(g_kv_a), wkv, wvt, invf,
                              conv_w[0].astype(F32), row(conv_b), row(conv_ln_g), row(conv_ln_b))
    attn = _mla_attn(qt, k, vt)
    return _mixer_out(x, attn, c2, p[0], w_o[0].astype(BF16), row(g_mix_post), row(g_ffn_pre),
                      w_ffn_gate[0].astype(BF16), w_ffn_up[0].astype(BF16), ffn_dw_w[0].astype(F32),
                      row(ffn_dw_b), w_ffn_down[0].astype(BF16), row(g_ffn_post),
                      w_ple_proj[0].astype(BF16), row(g_ple), w_ple_gate[0].astype(BF16))
```

```python
import functools

import jax
import jax.numpy as jnp
from jax import lax
from jax.experimental import pallas as pl
from jax.experimental.pallas import tpu as pltpu

D_MODEL = 1024
PLE_DIM = 256
MLA_HEADS = 8
QK_NOPE_DIM = 64
QK_ROPE_DIM = 32
QK_DIM = QK_NOPE_DIM + QK_ROPE_DIM
V_HEAD_DIM = 64
Q_LORA_RANK = 384
KV_LORA_RANK = 256
ROPE_THETA = 10000.0
CONV_CHANNELS = 512
CONV_WIDTH = 31
D_FF = 2816
FFN_CONV_WIDTH = 3
NORM_EPS = 1e-6

LANES = 128
SUBLANES = 8
FFN_CHUNK = 256
FFN_AHEAD = 3
HEAD_SLAB = LANES
HALF_ROPE = QK_ROPE_DIM // 2
CONV_HALO = 32
FFN_HALO = 8
SCORE_SCALE = (QK_DIM ** -0.5) * 1.4426950408889634

TM_IN = 512
SUB_IN = 256
TM_OUT = 256
VMEM_LIMIT = 56 * 1024 * 1024

BF16 = jnp.bfloat16
F32 = jnp.float32
NT_DIMS = (((1,), (1,)), ((), ()))


def _rms(x, g):
    return x * lax.rsqrt(jnp.mean(x * x, axis=-1, keepdims=True) + NORM_EPS) * g


def _const_spec(shape):
    nd = len(shape)
    return pl.BlockSpec(shape, lambda *_: (0,) * nd, pipeline_mode=pl.Buffered(1))


def _conformer_conv(c, h, sub, convw_ref, convb_ref, lng_ref, lnb_ref, cext_ref, shift_ref):
    lo = h * sub
    cext_ref[CONV_HALO + lo:CONV_HALO + lo + sub, :] = c
    span = sub + CONV_HALO - SUBLANES
    for r in range(1, SUBLANES):
        shift_ref[r - 1, h, 0:span, :] = cext_ref[lo + r:lo + r + span, :]
    acc = jnp.broadcast_to(convb_ref[...], (sub, CONV_CHANNELS))
    for j in range(CONV_WIDTH):
        a, r = divmod(CONV_HALO - (CONV_WIDTH - 1) + j, SUBLANES)
        if r == 0:
            tap = cext_ref[lo + a * SUBLANES:lo + a * SUBLANES + sub, :]
        else:
            tap = shift_ref[r - 1, h, a * SUBLANES:a * SUBLANES + sub, :]
        acc = acc + convw_ref[j:j + 1, :] * tap
    mu = jnp.mean(acc, axis=-1, keepdims=True)
    xc = acc - mu
    y = xc * lax.rsqrt(jnp.mean(xc * xc, axis=-1, keepdims=True) + NORM_EPS) * lng_ref[...] + lnb_ref[...]
    return (y * jax.nn.sigmoid(y)).astype(BF16)


def _mixer_in_kernel(pos_ref, x_ref, g_pre_ref, w1_ref, gq_ref, wqt_ref, gkv_ref, wkv_ref, wvt_ref,
                     invf_ref, convw_ref, convb_ref, lng_ref, lnb_ref,
                     qt_ref, k_ref, vt_ref, c2_ref, cext_ref, shift_ref, *, tm, sub):
    o_kv = Q_LORA_RANK
    o_a = o_kv + KV_LORA_RANK
    o_g = o_a + CONV_CHANNELS
    o_kr = o_g + CONV_CHANNELS
    o_kp = o_kr + LANES

    @pl.when(pl.program_id(1) == 0)
    def _():
        cext_ref[0:CONV_HALO, :] = jnp.zeros((CONV_HALO, CONV_CHANNELS), F32)

    for h in range(tm // sub):
        rows = slice(h * sub, (h + 1) * sub)
        xn = _rms(x_ref[0, rows, :], g_pre_ref[...]).astype(BF16)
        proj = jnp.dot(xn, w1_ref[...], preferred_element_type=F32)

        c = proj[:, o_a:o_g] * jax.nn.sigmoid(proj[:, o_g:o_kr])
        c2_ref[0, rows, :] = _conformer_conv(c, h, sub, convw_ref, convb_ref, lng_ref, lnb_ref,
                                             cext_ref, shift_ref)

        qn = _rms(proj[:, :o_kv], gq_ref[...]).astype(BF16)
        kvn = _rms(proj[:, o_kv:o_a], gkv_ref[...]).astype(BF16)

        ang = invf_ref[...] * pos_ref[0, :, rows].astype(F32)
        cos_t = jnp.cos(ang)
        sin_t = jnp.sin(ang)

        qt = lax.dot_general(wqt_ref[...], qn, NT_DIMS, preferred_element_type=F32) * SCORE_SCALE
        zero_rows = jnp.zeros((HEAD_SLAB - QK_DIM, sub), BF16)
        for hd in range(MLA_HEADS):
            r0 = hd * QK_DIM
            o0 = hd * HEAD_SLAB
            x1 = qt[r0 + QK_NOPE_DIM:r0 + QK_NOPE_DIM + HALF_ROPE]
            x2 = qt[r0 + QK_NOPE_DIM + HALF_ROPE:r0 + QK_DIM]
            qt_ref[0, o0:o0 + QK_NOPE_DIM, rows] = qt[r0:r0 + QK_NOPE_DIM].astype(BF16)
            qt_ref[0, o0 + QK_NOPE_DIM:o0 + QK_NOPE_DIM + HALF_ROPE, rows] = (x1 * cos_t - x2 * sin_t).astype(BF16)
            qt_ref[0, o0 + QK_NOPE_DIM + HALF_ROPE:o0 + QK_DIM, rows] = (x1 * sin_t + x2 * cos_t).astype(BF16)
            qt_ref[0, o0 + QK_DIM:o0 + HEAD_SLAB, rows] = zero_rows

        pad_lo = jnp.zeros((QK_NOPE_DIM, sub), F32)
        pad_hi = jnp.zeros((HEAD_SLAB - QK_DIM, sub), F32)
        cos_tok = jnp.concatenate([pad_lo, cos_t, cos_t, pad_hi], axis=0).T
        sin_tok = jnp.concatenate([pad_lo, sin_t, sin_t, pad_hi], axis=0).T
        k_rope = proj[:, o_kr:o_kp] * cos_tok + proj[:, o_kp:o_kp + LANES] * sin_tok

        kfull = jnp.dot(kvn, wkv_ref[...], preferred_element_type=F32)
        lane = lax.broadcasted_iota(jnp.int32, (sub, HEAD_SLAB), 1)
        for hd in range(MLA_HEADS):
            slab = kfull[:, hd * HEAD_SLAB:(hd + 1) * HEAD_SLAB]
            k_ref[0, rows, hd * HEAD_SLAB:(hd + 1) * HEAD_SLAB] = (
                jnp.where(lane < QK_NOPE_DIM, slab, k_rope).astype(BF16))

        vt = lax.dot_general(wvt_ref[...], kvn, NT_DIMS, preferred_element_type=F32)
        vt_ref[0, 0, :, rows] = vt.astype(BF16)

    cext_ref[0:CONV_HALO, :] = cext_ref[tm:tm + CONV_HALO, :]


def _mixer_in(pos3, x, g_pre, w1, gq, wqt, gkv, wkv, wvt, invf, convw, convb, lng, lnb):
    b, s, d = x.shape
    tm, sub = TM_IN, SUB_IN
    nt = s // tm
    n1 = w1.shape[1]
    chan = _const_spec((1, CONV_CHANNELS))
    return pl.pallas_call(
        functools.partial(_mixer_in_kernel, tm=tm, sub=sub),
        grid=(b, nt),
        in_specs=[
            pl.BlockSpec((1, 1, tm), lambda bi, i: (bi, 0, i)),
            pl.BlockSpec((1, tm, d), lambda bi, i: (bi, i, 0)),
            _const_spec((1, d)),
            _const_spec((d, n1)),
            _const_spec((1, Q_LORA_RANK)),
            _const_spec(wqt.shape),
            _const_spec((1, KV_LORA_RANK)),
            _const_spec(wkv.shape),
            _const_spec(wvt.shape),
            _const_spec(invf.shape),
            _const_spec(convw.shape), chan, chan, chan,
        ],
        out_specs=[
            pl.BlockSpec((1, MLA_HEADS * HEAD_SLAB, tm), lambda bi, i: (bi, 0, i)),
            pl.BlockSpec((1, tm, MLA_HEADS * HEAD_SLAB), lambda bi, i: (bi, i, 0)),
            pl.BlockSpec((1, 1, MLA_HEADS * V_HEAD_DIM, tm), lambda bi, i: (bi, i, 0, 0)),
            pl.BlockSpec((1, tm, CONV_CHANNELS), lambda bi, i: (bi, i, 0)),
        ],
        out_shape=[
            jax.ShapeDtypeStruct((b, MLA_HEADS * HEAD_SLAB, s), BF16),
            jax.ShapeDtypeStruct((b, s, MLA_HEADS * HEAD_SLAB), BF16),
            jax.ShapeDtypeStruct((b, nt, MLA_HEADS * V_HEAD_DIM, tm), BF16),
            jax.ShapeDtypeStruct((b, s, CONV_CHANNELS), BF16),
        ],
        scratch_shapes=[pltpu.VMEM((CONV_HALO + tm, CONV_CHANNELS), F32),
                        pltpu.VMEM((SUBLANES - 1, tm // sub, CONV_HALO + sub, CONV_CHANNELS), F32)],
        compiler_params=pltpu.CompilerParams(
            dimension_semantics=("arbitrary", "arbitrary"), vmem_limit_bytes=VMEM_LIMIT),
        name="mixer_in",
    )(pos3, x, g_pre, w1, gq, wqt, gkv, wkv, wvt, invf, convw, convb, lng, lnb)


HEADS_PER_STEP = 8
L_ROWS = 16
GAP_LIMIT = 16.0


def _attn_kernel(qt_ref, k_ref, vt_ref, o_ref, m_scr, acc_scr, gap_scr, *, t):
    qi = pl.program_id(2)
    heads = range(HEADS_PER_STEP)
    ones = jnp.ones((L_ROWS, t), BF16)

    def scores(e, j, masked):
        kj = k_ref[0, pl.ds(pl.multiple_of(j * t, t), t), e * HEAD_SLAB:(e + 1) * HEAD_SLAB]
        s = jnp.dot(kj, qt_ref[0, e * HEAD_SLAB:(e + 1) * HEAD_SLAB, :], preferred_element_type=F32)
        if masked:
            row = lax.broadcasted_iota(jnp.int32, (t, t), 0)
            col = lax.broadcasted_iota(jnp.int32, (t, t), 1)
            s = jnp.where(row <= col, s, -jnp.inf)
        return s

    def v_aug(e, j):
        return jnp.concatenate([vt_ref[0, j, e * V_HEAD_DIM:(e + 1) * V_HEAD_DIM, :], ones], axis=0)

    def two_pass_finish(e, j, s):
        m = m_scr[e]
        m_new = jnp.maximum(m, jnp.max(s, axis=0, keepdims=True))
        p = jnp.exp2(s - m_new).astype(BF16)
        acc_scr[e] = jnp.exp2(m - m_new) * acc_scr[e] + jnp.dot(v_aug(e, j), p, preferred_element_type=F32)
        m_scr[e] = m_new

    def one_pass_finish(e, j, s):
        m = m_scr[e]
        p = jnp.exp2(s - m).astype(BF16)
        cm = jnp.max(s, axis=0, keepdims=True)
        m_new = jnp.maximum(m, cm)
        gap_scr[e] = jnp.maximum(gap_scr[e], cm - m)
        acc_scr[e] = jnp.exp2(m - m_new) * (acc_scr[e] + jnp.dot(v_aug(e, j), p, preferred_element_type=F32))
        m_scr[e] = m_new

    def reset():
        for e in heads:
            m_scr[e] = jnp.full((1, t), -jnp.inf, F32)
            acc_scr[e] = jnp.zeros((V_HEAD_DIM + L_ROWS, t), F32)
            gap_scr[e] = jnp.zeros((1, t), F32)

    def write_out():
        outs = [acc_scr[e, :V_HEAD_DIM, :] * (1.0 / acc_scr[e, V_HEAD_DIM:V_HEAD_DIM + 1, :]) for e in heads]
        o_ref[0] = jnp.concatenate(outs, axis=0).T.astype(BF16)

    def all_heads(j, masked, finish):
        s_next = scores(0, j, masked)
        for e in heads:
            s = s_next
            if e + 1 < HEADS_PER_STEP:
                s_next = scores(e + 1, j, masked)
            finish(e, j, s)

    def run(finish):
        reset()
        all_heads(qi, True, two_pass_finish)

        def body(j, carry):
            all_heads(j, False, finish)
            return carry

        lax.fori_loop(0, qi, body, 0)
        write_out()

    run(one_pass_finish)
    worst_gap = functools.reduce(jnp.maximum, [jnp.max(gap_scr[e]) for e in heads])

    @pl.when(worst_gap > GAP_LIMIT)
    def _():
        run(two_pass_finish)


def _mla_attn(qt, k, vt):
    b, _, s = qt.shape
    t = TM_IN
    nt = s // t
    groups = MLA_HEADS // HEADS_PER_STEP
    return pl.pallas_call(
        functools.partial(_attn_kernel, t=t),
        grid=(b, groups, nt),
        in_specs=[
            pl.BlockSpec((1, HEADS_PER_STEP * HEAD_SLAB, t), lambda bi, p, qi: (bi, p, qi)),
            pl.BlockSpec((1, s, HEADS_PER_STEP * HEAD_SLAB), lambda bi, p, qi: (bi, 0, p)),
            pl.BlockSpec((1, nt, HEADS_PER_STEP * V_HEAD_DIM, t), lambda bi, p, qi: (bi, 0, p, 0)),
        ],
        out_specs=pl.BlockSpec((1, t, HEADS_PER_STEP * V_HEAD_DIM), lambda bi, p, qi: (bi, qi, p)),
        out_shape=jax.ShapeDtypeStruct((b, s, MLA_HEADS * V_HEAD_DIM), BF16),
        scratch_shapes=[pltpu.VMEM((HEADS_PER_STEP, 1, t), F32),
                        pltpu.VMEM((HEADS_PER_STEP, V_HEAD_DIM + L_ROWS, t), F32),
                        pltpu.VMEM((HEADS_PER_STEP, 1, t), F32)],
        compiler_params=pltpu.CompilerParams(
            dimension_semantics=("arbitrary", "arbitrary", "arbitrary"), vmem_limit_bytes=VMEM_LIMIT),
        name="mla_attn",
    )(qt, k, vt)


def _gelu_tanh(x):
    return x * (0.5 * (1.0 + jnp.tanh(0.7978845608028654 * (x + 0.044715 * (x * x * x)))))


def _mixer_out_kernel(x_ref, attn_ref, c2_ref, p_ref, wo_ref, g_mpost_ref, g_fpre_ref, wg_ref, wu_ref,
                      dww_ref, dwb_ref, wd_ref, g_fpost_ref, wpp_ref, g_ple_ref, wpg_ref,
                      o_ref, gcarry_ref, *, tm):
    mix_in = jnp.concatenate([attn_ref[0], c2_ref[0]], axis=-1)
    mix = jnp.dot(mix_in, wo_ref[...], preferred_element_type=F32)
    ple = jnp.dot(p_ref[0].astype(BF16), wpp_ref[...], preferred_element_type=F32)
    h1 = x_ref[0] + _rms(mix, g_mpost_ref[...])
    hn = _rms(h1, g_fpre_ref[...]).astype(BF16)

    @pl.when(pl.program_id(1) == 0)
    def _():
        gcarry_ref[...] = jnp.zeros((FFN_HALO, D_FF), F32)

    def gate_up(c0):
        cols = slice(c0, c0 + FFN_CHUNK)
        return (jnp.dot(hn, wg_ref[:, cols], preferred_element_type=F32),
                jnp.dot(hn, wu_ref[:, cols], preferred_element_type=F32))

    ffn = jnp.zeros((tm, D_MODEL), F32)
    starts = list(range(0, D_FF, FFN_CHUNK))
    pending = [gate_up(c0) for c0 in starts[:FFN_AHEAD]]
    for n, c0 in enumerate(starts):
        cols = slice(c0, c0 + FFN_CHUNK)
        gate, up = pending.pop(0)
        if n + FFN_AHEAD < len(starts):
            pending.append(gate_up(starts[n + FFN_AHEAD]))
        gext = jnp.concatenate([gcarry_ref[:, cols], gate], axis=0)
        g = (dwb_ref[:, cols] + dww_ref[0:1, cols] * gext[FFN_HALO - 2:FFN_HALO - 2 + tm]
             + dww_ref[1:2, cols] * gext[FFN_HALO - 1:FFN_HALO - 1 + tm] + dww_ref[2:3, cols] * gate)
        gcarry_ref[:, cols] = gate[tm - FFN_HALO:]
        act = (_gelu_tanh(g) * up).astype(BF16)
        ffn = ffn + jnp.dot(act, wd_ref[cols, :], preferred_element_type=F32)
    h2 = h1 + _rms(ffn, g_fpost_ref[...])
    gt = jax.nn.sigmoid(jnp.dot(h2.astype(BF16), wpg_ref[...], preferred_element_type=F32))
    o_ref[0] = h2 + gt * _rms(ple, g_ple_ref[...])


def _mixer_out(x, attn, c2, p, wo, g_mpost, g_fpre, wg, wu, dww, dwb, wd, g_fpost, wpp, g_ple, wpg):
    b, s, d = x.shape
    tm = TM_OUT
    nt = s // tm
    tok = lambda w: pl.BlockSpec((1, tm, w), lambda bi, i: (bi, i, 0))
    return pl.pallas_call(
        functools.partial(_mixer_out_kernel, tm=tm),
        grid=(b, nt),
        in_specs=[
            tok(d), tok(MLA_HEADS * V_HEAD_DIM), tok(CONV_CHANNELS), tok(PLE_DIM),
            _const_spec(wo.shape), _const_spec((1, d)), _const_spec((1, d)),
            _const_spec(wg.shape), _const_spec(wu.shape), _const_spec(dww.shape), _const_spec((1, D_FF)),
            _const_spec(wd.shape), _const_spec((1, d)), _const_spec(wpp.shape), _const_spec((1, d)),
            _const_spec(wpg.shape),
        ],
        out_specs=tok(d),
        out_shape=jax.ShapeDtypeStruct((b, s, d), F32),
        scratch_shapes=[pltpu.VMEM((FFN_HALO, D_FF), F32)],
        compiler_params=pltpu.CompilerParams(
            dimension_semantics=("arbitrary", "arbitrary"), vmem_limit_bytes=VMEM_LIMIT),
        name="mixer_out",
    )(x, attn, c2, p, wo, g_mpost, g_fpre, wg, wu, dww, dwb, wd, g_fpost, wpp, g_ple, wpg)


def kernel(x, p, positions, g_mix_pre, w_in, g_q_a, w_q_b, g_kv_a, w_kv_b, conv_w, conv_b, conv_ln_g, conv_ln_b, w_o, g_mix_post, g_ffn_pre, w_ffn_gate, w_ffn_up, ffn_dw_w, ffn_dw_b, w_ffn_down, g_ffn_post, w_ple_proj, g_ple, w_ple_gate):
    b, s, d = x.shape
    assert w_in.shape[0] == 1 and (b, s, d) == (x.shape[0], x.shape[1], D_MODEL)
    assert s % TM_IN == 0 and s % TM_OUT == 0
    row = lambda v: v[0][None, :].astype(F32)

    w_in0 = w_in[0]
    o_kr = Q_LORA_RANK + KV_LORA_RANK
    o_conv = o_kr + QK_ROPE_DIM
    w_kr = w_in0[:, o_kr:o_conv]
    w_kp = jnp.concatenate([-w_kr[:, HALF_ROPE:], w_kr[:, :HALF_ROPE]], axis=1)
    zc = lambda n: jnp.zeros((d, n), F32)
    slab = lambda w: jnp.concatenate([zc(QK_NOPE_DIM), w, zc(HEAD_SLAB - QK_DIM)], axis=1)
    w1 = jnp.concatenate([w_in0[:, :o_kr], w_in0[:, o_conv:], slab(w_kr), slab(w_kp)], axis=1).astype(BF16)

    wqt = w_q_b[0].T.astype(BF16)
    wkv = w_kv_b[0].astype(BF16)
    wvt = (w_kv_b[0].reshape(KV_LORA_RANK, MLA_HEADS, QK_NOPE_DIM + V_HEAD_DIM)[:, :, QK_NOPE_DIM:]
           .reshape(KV_LORA_RANK, MLA_HEADS * V_HEAD_DIM).T.astype(BF16))
    invf = (ROPE_THETA ** (-jnp.arange(0, QK_ROPE_DIM, 2, dtype=F32) / QK_ROPE_DIM))[:, None]
    pos3 = positions.reshape(b, 1, s)

    qt, k, vt, c2 = _mixer_in(pos3, x, row(g_mix_pre), w1, row(g_q_a), wqt, row(g_kv_a), wkv, wvt, invf,
                              conv_w[0].astype(F32), row(conv_b), row(conv_ln_g), row(conv_ln_b))
    attn = _mla_attn(qt, k, vt)
    return _mixer_out(x, attn, c2, p[0], w_o[0].astype(BF16), row(g_mix_post), row(g_ffn_pre),
                      w_ffn_gate[0].astype(BF16), w_ffn_up[0].astype(BF16), ffn_dw_w[0].astype(F32),
                      row(ffn_dw_b), w_ffn_down[0].astype(BF16), row(g_ffn_post),
                      w_ple_proj[0].astype(BF16), row(g_ple), w_ple_gate[0].astype(BF16))
```

```python
import functools

import jax
import jax.numpy as jnp
from jax import lax
from jax.experimental import pallas as pl
from jax.experimental.pallas import tpu as pltpu

D_MODEL = 1024
PLE_DIM = 256
MLA_HEADS = 8
QK_NOPE_DIM = 64
QK_ROPE_DIM = 32
QK_DIM = QK_NOPE_DIM + QK_ROPE_DIM
V_HEAD_DIM = 64
Q_LORA_RANK = 384
KV_LORA_RANK = 256
ROPE_THETA = 10000.0
CONV_CHANNELS = 512
CONV_WIDTH = 31
D_FF = 2816
FFN_CONV_WIDTH = 3
NORM_EPS = 1e-6

LANES = 128
SUBLANES = 8
FFN_CHUNK = 256
FFN_AHEAD = 3
HEAD_SLAB = LANES
HALF_ROPE = QK_ROPE_DIM // 2
CONV_HALO = 32
FFN_HALO = 8
SCORE_SCALE = (QK_DIM ** -0.5) * 1.4426950408889634

TM_IN = 512
SUB_IN = 256
TM_OUT = 512
SUB_OUT = 256
VMEM_LIMIT = 56 * 1024 * 1024

BF16 = jnp.bfloat16
F32 = jnp.float32
NT_DIMS = (((1,), (1,)), ((), ()))


def _rms(x, g):
    return x * lax.rsqrt(jnp.mean(x * x, axis=-1, keepdims=True) + NORM_EPS) * g


def _const_spec(shape):
    nd = len(shape)
    return pl.BlockSpec(shape, lambda *_: (0,) * nd, pipeline_mode=pl.Buffered(1))


def _conformer_conv(c, h, sub, convw_ref, convb_ref, lng_ref, lnb_ref, cext_ref, shift_ref):
    lo = h * sub
    cext_ref[CONV_HALO + lo:CONV_HALO + lo + sub, :] = c
    span = sub + CONV_HALO - SUBLANES
    for r in range(1, SUBLANES):
        shift_ref[r - 1, h, 0:span, :] = cext_ref[lo + r:lo + r + span, :]
    acc = jnp.broadcast_to(convb_ref[...], (sub, CONV_CHANNELS))
    for j in range(CONV_WIDTH):
        a, r = divmod(CONV_HALO - (CONV_WIDTH - 1) + j, SUBLANES)
        if r == 0:
            tap = cext_ref[lo + a * SUBLANES:lo + a * SUBLANES + sub, :]
        else:
            tap = shift_ref[r - 1, h, a * SUBLANES:a * SUBLANES + sub, :]
        acc = acc + convw_ref[j:j + 1, :] * tap
    mu = jnp.mean(acc, axis=-1, keepdims=True)
    xc = acc - mu
    y = xc * lax.rsqrt(jnp.mean(xc * xc, axis=-1, keepdims=True) + NORM_EPS) * lng_ref[...] + lnb_ref[...]
    return (y * jax.nn.sigmoid(y)).astype(BF16)


def _mixer_in_kernel(pos_ref, x_ref, g_pre_ref, w1_ref, gq_ref, wqt_ref, gkv_ref, wkv_ref, wvt_ref,
                     invf_ref, convw_ref, convb_ref, lng_ref, lnb_ref,
                     qt_ref, k_ref, vt_ref, c2_ref, cext_ref, shift_ref, *, tm, sub):
    o_kv = Q_LORA_RANK
    o_a = o_kv + KV_LORA_RANK
    o_g = o_a + CONV_CHANNELS
    o_kr = o_g + CONV_CHANNELS
    o_kp = o_kr + LANES

    @pl.when(pl.program_id(1) == 0)
    def _():
        cext_ref[0:CONV_HALO, :] = jnp.zeros((CONV_HALO, CONV_CHANNELS), F32)

    for h in range(tm // sub):
        rows = slice(h * sub, (h + 1) * sub)
        xn = _rms(x_ref[0, rows, :], g_pre_ref[...]).astype(BF16)
        proj = jnp.dot(xn, w1_ref[...], preferred_element_type=F32)

        c = proj[:, o_a:o_g] * jax.nn.sigmoid(proj[:, o_g:o_kr])
        c2_ref[0, rows, :] = _conformer_conv(c, h, sub, convw_ref, convb_ref, lng_ref, lnb_ref,
                                             cext_ref, shift_ref)

        qn = _rms(proj[:, :o_kv], gq_ref[...]).astype(BF16)
        kvn = _rms(proj[:, o_kv:o_a], gkv_ref[...]).astype(BF16)

        ang = invf_ref[...] * pos_ref[0, :, rows].astype(F32)
        cos_t = jnp.cos(ang)
        sin_t = jnp.sin(ang)

        qt = lax.dot_general(wqt_ref[...], qn, NT_DIMS, preferred_element_type=F32) * SCORE_SCALE
        zero_rows = jnp.zeros((HEAD_SLAB - QK_DIM, sub), BF16)
        for hd in range(MLA_HEADS):
            r0 = hd * QK_DIM
            o0 = hd * HEAD_SLAB
            x1 = qt[r0 + QK_NOPE_DIM:r0 + QK_NOPE_DIM + HALF_ROPE]
            x2 = qt[r0 + QK_NOPE_DIM + HALF_ROPE:r0 + QK_DIM]
            qt_ref[0, o0:o0 + QK_NOPE_DIM, rows] = qt[r0:r0 + QK_NOPE_DIM].astype(BF16)
            qt_ref[0, o0 + QK_NOPE_DIM:o0 + QK_NOPE_DIM + HALF_ROPE, rows] = (x1 * cos_t - x2 * sin_t).astype(BF16)
            qt_ref[0, o0 + QK_NOPE_DIM + HALF_ROPE:o0 + QK_DIM, rows] = (x1 * sin_t + x2 * cos_t).astype(BF16)
            qt_ref[0, o0 + QK_DIM:o0 + HEAD_SLAB, rows] = zero_rows

        pad_lo = jnp.zeros((QK_NOPE_DIM, sub), F32)
        pad_hi = jnp.zeros((HEAD_SLAB - QK_DIM, sub), F32)
        cos_tok = jnp.concatenate([pad_lo, cos_t, cos_t, pad_hi], axis=0).T
        sin_tok = jnp.concatenate([pad_lo, sin_t, sin_t, pad_hi], axis=0).T
        k_rope = proj[:, o_kr:o_kp] * cos_tok + proj[:, o_kp:o_kp + LANES] * sin_tok

        kfull = jnp.dot(kvn, wkv_ref[...], preferred_element_type=F32)
        lane = lax.broadcasted_iota(jnp.int32, (sub, HEAD_SLAB), 1)
        for hd in range(MLA_HEADS):
            slab = kfull[:, hd * HEAD_SLAB:(hd + 1) * HEAD_SLAB]
            k_ref[0, rows, hd * HEAD_SLAB:(hd + 1) * HEAD_SLAB] = (
                jnp.where(lane < QK_NOPE_DIM, slab, k_rope).astype(BF16))

        vt = lax.dot_general(wvt_ref[...], kvn, NT_DIMS, preferred_element_type=F32)
        vt_ref[0, 0, :, rows] = vt.astype(BF16)

    cext_ref[0:CONV_HALO, :] = cext_ref[tm:tm + CONV_HALO, :]


def _mixer_in(pos3, x, g_pre, w1, gq, wqt, gkv, wkv, wvt, invf, convw, convb, lng, lnb):
    b, s, d = x.shape
    tm, sub = TM_IN, SUB_IN
    nt = s // tm
    n1 = w1.shape[1]
    chan = _const_spec((1, CONV_CHANNELS))
    return pl.pallas_call(
        functools.partial(_mixer_in_kernel, tm=tm, sub=sub),
        grid=(b, nt),
        in_specs=[
            pl.BlockSpec((1, 1, tm), lambda bi, i: (bi, 0, i)),
            pl.BlockSpec((1, tm, d), lambda bi, i: (bi, i, 0)),
            _const_spec((1, d)),
            _const_spec((d, n1)),
            _const_spec((1, Q_LORA_RANK)),
            _const_spec(wqt.shape),
            _const_spec((1, KV_LORA_RANK)),
            _const_spec(wkv.shape),
            _const_spec(wvt.shape),
            _const_spec(invf.shape),
            _const_spec(convw.shape), chan, chan, chan,
        ],
        out_specs=[
            pl.BlockSpec((1, MLA_HEADS * HEAD_SLAB, tm), lambda bi, i: (bi, 0, i)),
            pl.BlockSpec((1, tm, MLA_HEADS * HEAD_SLAB), lambda bi, i: (bi, i, 0)),
            pl.BlockSpec((1, 1, MLA_HEADS * V_HEAD_DIM, tm), lambda bi, i: (bi, i, 0, 0)),
            pl.BlockSpec((1, tm, CONV_CHANNELS), lambda bi, i: (bi, i, 0)),
        ],
        out_shape=[
            jax.ShapeDtypeStruct((b, MLA_HEADS * HEAD_SLAB, s), BF16),
            jax.ShapeDtypeStruct((b, s, MLA_HEADS * HEAD_SLAB), BF16),
            jax.ShapeDtypeStruct((b, nt, MLA_HEADS * V_HEAD_DIM, tm), BF16),
            jax.ShapeDtypeStruct((b, s, CONV_CHANNELS), BF16),
        ],
        scratch_shapes=[pltpu.VMEM((CONV_HALO + tm, CONV_CHANNELS), F32),
                        pltpu.VMEM((SUBLANES - 1, tm // sub, CONV_HALO + sub, CONV_CHANNELS), F32)],
        compiler_params=pltpu.CompilerParams(
            dimension_semantics=("arbitrary", "arbitrary"), vmem_limit_bytes=VMEM_LIMIT),
        name="mixer_in",
    )(pos3, x, g_pre, w1, gq, wqt, gkv, wkv, wvt, invf, convw, convb, lng, lnb)


HEADS_PER_STEP = 8
L_ROWS = 16
GAP_LIMIT = 16.0


def _attn_kernel(qt_ref, k_ref, vt_ref, o_ref, m_scr, acc_scr, gap_scr, *, t):
    qi = pl.program_id(2)
    heads = range(HEADS_PER_STEP)
    ones = jnp.ones((L_ROWS, t), BF16)

    def scores(e, j, masked):
        kj = k_ref[0, pl.ds(pl.multiple_of(j * t, t), t), e * HEAD_SLAB:(e + 1) * HEAD_SLAB]
        s = jnp.dot(kj, qt_ref[0, e * HEAD_SLAB:(e + 1) * HEAD_SLAB, :], preferred_element_type=F32)
        if masked:
            row = lax.broadcasted_iota(jnp.int32, (t, t), 0)
            col = lax.broadcasted_iota(jnp.int32, (t, t), 1)
            s = jnp.where(row <= col, s, -jnp.inf)
        return s

    def v_aug(e, j):
        return jnp.concatenate([vt_ref[0, j, e * V_HEAD_DIM:(e + 1) * V_HEAD_DIM, :], ones], axis=0)

    def two_pass_finish(e, j, s):
        m = m_scr[e]
        m_new = jnp.maximum(m, jnp.max(s, axis=0, keepdims=True))
        p = jnp.exp2(s - m_new).astype(BF16)
        acc_scr[e] = jnp.exp2(m - m_new) * acc_scr[e] + jnp.dot(v_aug(e, j), p, preferred_element_type=F32)
        m_scr[e] = m_new

    def one_pass_finish(e, j, s):
        m = m_scr[e]
        p = jnp.exp2(s - m).astype(BF16)
        cm = jnp.max(s, axis=0, keepdims=True)
        m_new = jnp.maximum(m, cm)
        gap_scr[e] = jnp.maximum(gap_scr[e], cm - m)
        acc_scr[e] = jnp.exp2(m - m_new) * (acc_scr[e] + jnp.dot(v_aug(e, j), p, preferred_element_type=F32))
        m_scr[e] = m_new

    def reset():
        for e in heads:
            m_scr[e] = jnp.full((1, t), -jnp.inf, F32)
            acc_scr[e] = jnp.zeros((V_HEAD_DIM + L_ROWS, t), F32)
            gap_scr[e] = jnp.zeros((1, t), F32)

    def write_out():
        outs = [acc_scr[e, :V_HEAD_DIM, :] * (1.0 / acc_scr[e, V_HEAD_DIM:V_HEAD_DIM + 1, :]) for e in heads]
        o_ref[0] = jnp.concatenate(outs, axis=0).T.astype(BF16)

    def all_heads(j, masked, finish):
        s_next = scores(0, j, masked)
        for e in heads:
            s = s_next
            if e + 1 < HEADS_PER_STEP:
                s_next = scores(e + 1, j, masked)
            finish(e, j, s)

    def run(finish):
        reset()
        all_heads(qi, True, two_pass_finish)

        def body(j, carry):
            all_heads(j, False, finish)
            return carry

        lax.fori_loop(0, qi, body, 0)
        write_out()

    run(one_pass_finish)
    worst_gap = functools.reduce(jnp.maximum, [jnp.max(gap_scr[e]) for e in heads])

    @pl.when(worst_gap > GAP_LIMIT)
    def _():
        run(two_pass_finish)


def _mla_attn(qt, k, vt):
    b, _, s = qt.shape
    t = TM_IN
    nt = s // t
    groups = MLA_HEADS // HEADS_PER_STEP
    return pl.pallas_call(
        functools.partial(_attn_kernel, t=t),
        grid=(b, groups, nt),
        in_specs=[
            pl.BlockSpec((1, HEADS_PER_STEP * HEAD_SLAB, t), lambda bi, p, qi: (bi, p, qi)),
            pl.BlockSpec((1, s, HEADS_PER_STEP * HEAD_SLAB), lambda bi, p, qi: (bi, 0, p)),
            pl.BlockSpec((1, nt, HEADS_PER_STEP * V_HEAD_DIM, t), lambda bi, p, qi: (bi, 0, p, 0)),
        ],
        out_specs=pl.BlockSpec((1, t, HEADS_PER_STEP * V_HEAD_DIM), lambda bi, p, qi: (bi, qi, p)),
        out_shape=jax.ShapeDtypeStruct((b, s, MLA_HEADS * V_HEAD_DIM), BF16),
        scratch_shapes=[pltpu.VMEM((HEADS_PER_STEP, 1, t), F32),
                        pltpu.VMEM((HEADS_PER_STEP, V_HEAD_DIM + L_ROWS, t), F32),
                        pltpu.VMEM((HEADS_PER_STEP, 1, t), F32)],
        compiler_params=pltpu.CompilerParams(
            dimension_semantics=("arbitrary", "arbitrary", "arbitrary"), vmem_limit_bytes=VMEM_LIMIT),
        name="mla_attn",
    )(qt, k, vt)


def _gelu_tanh(x):
    return x * (0.5 * (1.0 + jnp.tanh(0.7978845608028654 * (x + 0.044715 * (x * x * x)))))


def _mixer_out_kernel(x_ref, attn_ref, c2_ref, p_ref, wo_ref, g_mpost_ref, g_fpre_ref, wg_ref, wu_ref,
                      dww_ref, dwb_ref, wd_ref, g_fpost_ref, wpp_ref, g_ple_ref, wpg_ref,
                      o_ref, gcarry_ref, *, tm, sub):
    subs = range(tm // sub)
    rows = [slice(h * sub, (h + 1) * sub) for h in subs]

    @pl.when(pl.program_id(1) == 0)
    def _():
        gcarry_ref[...] = jnp.zeros((FFN_HALO, D_FF), F32)

    mix = [jnp.dot(jnp.concatenate([attn_ref[0, r, :], c2_ref[0, r, :]], axis=-1), wo_ref[...],
                   preferred_element_type=F32) for r in rows]
    ple = [jnp.dot(p_ref[0, r, :].astype(BF16), wpp_ref[...], preferred_element_type=F32) for r in rows]
    h1 = [x_ref[0, rows[h], :] + _rms(mix[h], g_mpost_ref[...]) for h in subs]
    hn = [_rms(h1[h], g_fpre_ref[...]).astype(BF16) for h in subs]

    starts = list(range(0, D_FF, FFN_CHUNK))

    def gate_up(h, c0):
        cols = slice(c0, min(c0 + FFN_CHUNK, D_FF))
        return (jnp.dot(hn[h], wg_ref[:, cols], preferred_element_type=F32),
                jnp.dot(hn[h], wu_ref[:, cols], preferred_element_type=F32))

    def ffn_chunks(h, pending):
        ffn = jnp.zeros((sub, D_MODEL), F32)
        for n, c0 in enumerate(starts):
            cols = slice(c0, min(c0 + FFN_CHUNK, D_FF))
            gate, up = pending.pop(0)
            if n + FFN_AHEAD < len(starts):
                pending.append(gate_up(h, starts[n + FFN_AHEAD]))
            gext = jnp.concatenate([gcarry_ref[:, cols], gate], axis=0)
            g = (dwb_ref[:, cols] + dww_ref[0:1, cols] * gext[FFN_HALO - 2:FFN_HALO - 2 + sub]
                 + dww_ref[1:2, cols] * gext[FFN_HALO - 1:FFN_HALO - 1 + sub] + dww_ref[2:3, cols] * gate)
            gcarry_ref[:, cols] = gate[sub - FFN_HALO:]
            act = (_gelu_tanh(g) * up).astype(BF16)
            ffn = ffn + jnp.dot(act, wd_ref[cols, :], preferred_element_type=F32)
        return ffn

    pending = [gate_up(0, c0) for c0 in starts[:FFN_AHEAD]]
    for h in subs:
        ffn = ffn_chunks(h, pending)
        if h + 1 < len(subs):
            pending = [gate_up(h + 1, c0) for c0 in starts[:FFN_AHEAD]]
        h2 = h1[h] + _rms(ffn, g_fpost_ref[...])
        gt = jax.nn.sigmoid(jnp.dot(h2.astype(BF16), wpg_ref[...], preferred_element_type=F32))
        o_ref[0, rows[h], :] = h2 + gt * _rms(ple[h], g_ple_ref[...])


def _mixer_out(x, attn, c2, p, wo, g_mpost, g_fpre, wg, wu, dww, dwb, wd, g_fpost, wpp, g_ple, wpg):
    b, s, d = x.shape
    tm, sub = TM_OUT, SUB_OUT
    nt = s // tm
    tok = lambda w: pl.BlockSpec((1, tm, w), lambda bi, i: (bi, i, 0))
    return pl.pallas_call(
        functools.partial(_mixer_out_kernel, tm=tm, sub=sub),
        grid=(b, nt),
        in_specs=[
            tok(d), tok(MLA_HEADS * V_HEAD_DIM), tok(CONV_CHANNELS), tok(PLE_DIM),
            _const_spec(wo.shape), _const_spec((1, d)), _const_spec((1, d)),
            _const_spec(wg.shape), _const_spec(wu.shape), _const_spec(dww.shape), _const_spec((1, D_FF)),
            _const_spec(wd.shape), _const_spec((1, d)), _const_spec(wpp.shape), _const_spec((1, d)),
            _const_spec(wpg.shape),
        ],
        out_specs=tok(d),
        out_shape=jax.ShapeDtypeStruct((b, s, d), F32),
        scratch_shapes=[pltpu.VMEM((FFN_HALO, D_FF), F32)],
        compiler_params=pltpu.CompilerParams(
            dimension_semantics=("arbitrary", "arbitrary"), vmem_limit_bytes=VMEM_LIMIT),
        name="mixer_out",
    )(x, attn, c2, p, wo, g_mpost, g_fpre, wg, wu, dww, dwb, wd, g_fpost, wpp, g_ple, wpg)


def kernel(x, p, positions, g_mix_pre, w_in, g_q_a, w_q_b, g_kv_a, w_kv_b, conv_w, conv_b, conv_ln_g, conv_ln_b, w_o, g_mix_post, g_ffn_pre, w_ffn_gate, w_ffn_up, ffn_dw_w, ffn_dw_b, w_ffn_down, g_ffn_post, w_ple_proj, g_ple, w_ple_gate):
    b, s, d = x.shape
    assert w_in.shape[0] == 1 and (b, s, d) == (x.shape[0], x.shape[1], D_MODEL)
    assert s % TM_IN == 0 and s % TM_OUT == 0
    row = lambda v: v[0][None, :].astype(F32)

    w_in0 = w_in[0]
    o_kr = Q_LORA_RANK + KV_LORA_RANK
    o_conv = o_kr + QK_ROPE_DIM
    w_kr = w_in0[:, o_kr:o_conv]
    w_kp = jnp.concatenate([-w_kr[:, HALF_ROPE:], w_kr[:, :HALF_ROPE]], axis=1)
    zc = lambda n: jnp.zeros((d, n), F32)
    slab = lambda w: jnp.concatenate([zc(QK_NOPE_DIM), w, zc(HEAD_SLAB - QK_DIM)], axis=1)
    w1 = jnp.concatenate([w_in0[:, :o_kr], w_in0[:, o_conv:], slab(w_kr), slab(w_kp)], axis=1).astype(BF16)

    wqt = w_q_b[0].T.astype(BF16)
    wkv = w_kv_b[0].astype(BF16)
    wvt = (w_kv_b[0].reshape(KV_LORA_RANK, MLA_HEADS, QK_NOPE_DIM + V_HEAD_DIM)[:, :, QK_NOPE_DIM:]
           .reshape(KV_LORA_RANK, MLA_HEADS * V_HEAD_DIM).T.astype(BF16))
    invf = (ROPE_THETA ** (-jnp.arange(0, QK_ROPE_DIM, 2, dtype=F32) / QK_ROPE_DIM))[:, None]
    pos3 = positions.reshape(b, 1, s)

    qt, k, vt, c2 = _mixer_in(pos3, x, row(g_mix_pre), w1, row(g_q_a), wqt, row(g_kv_a), wkv, wvt, invf,
                              conv_w[0].astype(F32), row(conv_b), row(conv_ln_g), row(conv_ln_b))
    attn = _mla_attn(qt, k, vt)
    return _mixer_out(x, attn, c2, p[0], w_o[0].astype(BF16), row(g_mix_post), row(g_ffn_pre),
                      w_ffn_gate[0].astype(BF16), w_ffn_up[0].astype(BF16), ffn_dw_w[0].astype(F32),
                      row(ffn_dw_b), w_ffn_down[0].astype(BF16), row(g_ffn_post),
                      w_ple_proj[0].astype(BF16), row(g_ple), w_ple_gate[0].astype(BF16))
```

```python
import functools

import jax
import jax.numpy as jnp
from jax import lax
from jax.experimental import pallas as pl
from jax.experimental.pallas import tpu as pltpu

D_MODEL = 1024
PLE_DIM = 256
MLA_HEADS = 8
QK_NOPE_DIM = 64
QK_ROPE_DIM = 32
QK_DIM = QK_NOPE_DIM + QK_ROPE_DIM
V_HEAD_DIM = 64
Q_LORA_RANK = 384
KV_LORA_RANK = 256
ROPE_THETA = 10000.0
CONV_CHANNELS = 512
CONV_WIDTH = 31
D_FF = 2816
FFN_CONV_WIDTH = 3
NORM_EPS = 1e-6

LANES = 128
SUBLANES = 8
FFN_CHUNK = 256
FFN_AHEAD = 3
HEAD_SLAB = LANES
HALF_ROPE = QK_ROPE_DIM // 2
CONV_HALO = 32
FFN_HALO = 8
SCORE_SCALE = (QK_DIM ** -0.5) * 1.4426950408889634

TM_IN = 1024
ATTN_TILE = 512
SUB_IN = 256
TM_OUT = 512
SUB_OUT = 256
VMEM_LIMIT = 56 * 1024 * 1024

BF16 = jnp.bfloat16
F32 = jnp.float32
NT_DIMS = (((1,), (1,)), ((), ()))


def _rms(x, g):
    return x * lax.rsqrt(jnp.mean(x * x, axis=-1, keepdims=True) + NORM_EPS) * g


def _const_spec(shape):
    nd = len(shape)
    return pl.BlockSpec(shape, lambda *_: (0,) * nd, pipeline_mode=pl.Buffered(1))


def _conformer_conv(c, h, sub, convw_ref, convb_ref, lng_ref, lnb_ref, cext_ref, shift_ref):
    lo = h * sub
    cext_ref[CONV_HALO + lo:CONV_HALO + lo + sub, :] = c
    span = sub + CONV_HALO - SUBLANES
    for r in range(1, SUBLANES):
        shift_ref[r - 1, h % 2, 0:span, :] = cext_ref[lo + r:lo + r + span, :]
    acc = jnp.broadcast_to(convb_ref[...], (sub, CONV_CHANNELS))
    for j in range(CONV_WIDTH):
        a, r = divmod(CONV_HALO - (CONV_WIDTH - 1) + j, SUBLANES)
        if r == 0:
            tap = cext_ref[lo + a * SUBLANES:lo + a * SUBLANES + sub, :]
        else:
            tap = shift_ref[r - 1, h % 2, a * SUBLANES:a * SUBLANES + sub, :]
        acc = acc + convw_ref[j:j + 1, :] * tap
    mu = jnp.mean(acc, axis=-1, keepdims=True)
    xc = acc - mu
    y = xc * lax.rsqrt(jnp.mean(xc * xc, axis=-1, keepdims=True) + NORM_EPS) * lng_ref[...] + lnb_ref[...]
    return (y * jax.nn.sigmoid(y)).astype(BF16)


def _mixer_in_kernel(pos_ref, x_ref, g_pre_ref, w1_ref, gq_ref, wqt_ref, gkv_ref, wkv_ref, wvt_ref,
                     invf_ref, convw_ref, convb_ref, lng_ref, lnb_ref,
                     qt_ref, k_ref, vt_ref, c2_ref, cext_ref, shift_ref, *, tm, sub):
    o_kv = Q_LORA_RANK
    o_a = o_kv + KV_LORA_RANK
    o_g = o_a + CONV_CHANNELS
    o_kr = o_g + CONV_CHANNELS
    o_kp = o_kr + LANES

    @pl.when(pl.program_id(1) == 0)
    def _():
        cext_ref[0:CONV_HALO, :] = jnp.zeros((CONV_HALO, CONV_CHANNELS), F32)

    for h in range(tm // sub):
        rows = slice(h * sub, (h + 1) * sub)
        xn = _rms(x_ref[0, rows, :], g_pre_ref[...]).astype(BF16)
        proj = jnp.dot(xn, w1_ref[...], preferred_element_type=F32)

        c = proj[:, o_a:o_g] * jax.nn.sigmoid(proj[:, o_g:o_kr])
        c2_ref[0, rows, :] = _conformer_conv(c, h, sub, convw_ref, convb_ref, lng_ref, lnb_ref,
                                             cext_ref, shift_ref)

        qn = _rms(proj[:, :o_kv], gq_ref[...]).astype(BF16)
        kvn = _rms(proj[:, o_kv:o_a], gkv_ref[...]).astype(BF16)

        ang = invf_ref[...] * pos_ref[0, :, rows].astype(F32)
        cos_t = jnp.cos(ang)
        sin_t = jnp.sin(ang)

        qt = lax.dot_general(wqt_ref[...], qn, NT_DIMS, preferred_element_type=F32) * SCORE_SCALE
        zero_rows = jnp.zeros((HEAD_SLAB - QK_DIM, sub), BF16)
        for hd in range(MLA_HEADS):
            r0 = hd * QK_DIM
            o0 = hd * HEAD_SLAB
            x1 = qt[r0 + QK_NOPE_DIM:r0 + QK_NOPE_DIM + HALF_ROPE]
            x2 = qt[r0 + QK_NOPE_DIM + HALF_ROPE:r0 + QK_DIM]
            qt_ref[0, o0:o0 + QK_NOPE_DIM, rows] = qt[r0:r0 + QK_NOPE_DIM].astype(BF16)
            qt_ref[0, o0 + QK_NOPE_DIM:o0 + QK_NOPE_DIM + HALF_ROPE, rows] = (x1 * cos_t - x2 * sin_t).astype(BF16)
            qt_ref[0, o0 + QK_NOPE_DIM + HALF_ROPE:o0 + QK_DIM, rows] = (x1 * sin_t + x2 * cos_t).astype(BF16)
            qt_ref[0, o0 + QK_DIM:o0 + HEAD_SLAB, rows] = zero_rows

        pad_lo = jnp.zeros((QK_NOPE_DIM, sub), F32)
        pad_hi = jnp.zeros((HEAD_SLAB - QK_DIM, sub), F32)
        cos_tok = jnp.concatenate([pad_lo, cos_t, cos_t, pad_hi], axis=0).T
        sin_tok = jnp.concatenate([pad_lo, sin_t, sin_t, pad_hi], axis=0).T
        k_rope = proj[:, o_kr:o_kp] * cos_tok + proj[:, o_kp:o_kp + LANES] * sin_tok

        kfull = jnp.dot(kvn, wkv_ref[...], preferred_element_type=F32)
        lane = lax.broadcasted_iota(jnp.int32, (sub, HEAD_SLAB), 1)
        for hd in range(MLA_HEADS):
            slab = kfull[:, hd * HEAD_SLAB:(hd + 1) * HEAD_SLAB]
            k_ref[0, rows, hd * HEAD_SLAB:(hd + 1) * HEAD_SLAB] = (
                jnp.where(lane < QK_NOPE_DIM, slab, k_rope).astype(BF16))

        vt = lax.dot_general(wvt_ref[...], kvn, NT_DIMS, preferred_element_type=F32)
        per_kv = ATTN_TILE // sub
        vt_ref[0, h // per_kv, :, (h % per_kv) * sub:(h % per_kv + 1) * sub] = vt.astype(BF16)

    cext_ref[0:CONV_HALO, :] = cext_ref[tm:tm + CONV_HALO, :]


def _mixer_in(pos3, x, g_pre, w1, gq, wqt, gkv, wkv, wvt, invf, convw, convb, lng, lnb):
    b, s, d = x.shape
    tm, sub = TM_IN, SUB_IN
    nt = s // tm
    n1 = w1.shape[1]
    chan = _const_spec((1, CONV_CHANNELS))
    return pl.pallas_call(
        functools.partial(_mixer_in_kernel, tm=tm, sub=sub),
        grid=(b, nt),
        in_specs=[
            pl.BlockSpec((1, 1, tm), lambda bi, i: (bi, 0, i)),
            pl.BlockSpec((1, tm, d), lambda bi, i: (bi, i, 0)),
            _const_spec((1, d)),
            _const_spec((d, n1)),
            _const_spec((1, Q_LORA_RANK)),
            _const_spec(wqt.shape),
            _const_spec((1, KV_LORA_RANK)),
            _const_spec(wkv.shape),
            _const_spec(wvt.shape),
            _const_spec(invf.shape),
            _const_spec(convw.shape), chan, chan, chan,
        ],
        out_specs=[
            pl.BlockSpec((1, MLA_HEADS * HEAD_SLAB, tm), lambda bi, i: (bi, 0, i)),
            pl.BlockSpec((1, tm, MLA_HEADS * HEAD_SLAB), lambda bi, i: (bi, i, 0)),
            pl.BlockSpec((1, tm // ATTN_TILE, MLA_HEADS * V_HEAD_DIM, ATTN_TILE), lambda bi, i: (bi, i, 0, 0)),
            pl.BlockSpec((1, tm, CONV_CHANNELS), lambda bi, i: (bi, i, 0)),
        ],
        out_shape=[
            jax.ShapeDtypeStruct((b, MLA_HEADS * HEAD_SLAB, s), BF16),
            jax.ShapeDtypeStruct((b, s, MLA_HEADS * HEAD_SLAB), BF16),
            jax.ShapeDtypeStruct((b, s // ATTN_TILE, MLA_HEADS * V_HEAD_DIM, ATTN_TILE), BF16),
            jax.ShapeDtypeStruct((b, s, CONV_CHANNELS), BF16),
        ],
        scratch_shapes=[pltpu.VMEM((CONV_HALO + tm, CONV_CHANNELS), F32),
                        pltpu.VMEM((SUBLANES - 1, 2, CONV_HALO + sub, CONV_CHANNELS), F32)],
        compiler_params=pltpu.CompilerParams(
            dimension_semantics=("arbitrary", "arbitrary"), vmem_limit_bytes=VMEM_LIMIT),
        name="mixer_in",
    )(pos3, x, g_pre, w1, gq, wqt, gkv, wkv, wvt, invf, convw, convb, lng, lnb)


HEADS_PER_STEP = 8
L_ROWS = 16
GAP_LIMIT = 16.0


def _attn_kernel(qt_ref, k_ref, vt_ref, o_ref, m_scr, acc_scr, gap_scr, *, t):
    qi = pl.program_id(2)
    heads = range(HEADS_PER_STEP)
    ones = jnp.ones((L_ROWS, t), BF16)

    def scores(e, j, masked):
        kj = k_ref[0, pl.ds(pl.multiple_of(j * t, t), t), e * HEAD_SLAB:(e + 1) * HEAD_SLAB]
        s = jnp.dot(kj, qt_ref[0, e * HEAD_SLAB:(e + 1) * HEAD_SLAB, :], preferred_element_type=F32)
        if masked:
            row = lax.broadcasted_iota(jnp.int32, (t, t), 0)
            col = lax.broadcasted_iota(jnp.int32, (t, t), 1)
            s = jnp.where(row <= col, s, -jnp.inf)
        return s

    def v_aug(e, j):
        return jnp.concatenate([vt_ref[0, j, e * V_HEAD_DIM:(e + 1) * V_HEAD_DIM, :], ones], axis=0)

    def two_pass_finish(e, j, s):
        m = m_scr[e]
        m_new = jnp.maximum(m, jnp.max(s, axis=0, keepdims=True))
        p = jnp.exp2(s - m_new).astype(BF16)
        acc_scr[e] = jnp.exp2(m - m_new) * acc_scr[e] + jnp.dot(v_aug(e, j), p, preferred_element_type=F32)
        m_scr[e] = m_new

    def one_pass_finish(e, j, s):
        m = m_scr[e]
        p = jnp.exp2(s - m).astype(BF16)
        cm = jnp.max(s, axis=0, keepdims=True)
        m_new = jnp.maximum(m, cm)
        gap_scr[e] = jnp.maximum(gap_scr[e], cm - m)
        acc_scr[e] = jnp.exp2(m - m_new) * (acc_scr[e] + jnp.dot(v_aug(e, j), p, preferred_element_type=F32))
        m_scr[e] = m_new

    def reset():
        for e in heads:
            m_scr[e] = jnp.full((1, t), -jnp.inf, F32)
            acc_scr[e] = jnp.zeros((V_HEAD_DIM + L_ROWS, t), F32)
            gap_scr[e] = jnp.zeros((1, t), F32)

    def write_out():
        outs = [acc_scr[e, :V_HEAD_DIM, :] * (1.0 / acc_scr[e, V_HEAD_DIM:V_HEAD_DIM + 1, :]) for e in heads]
        o_ref[0] = jnp.concatenate(outs, axis=0).T.astype(BF16)

    def all_heads(j, masked, finish):
        s_next = scores(0, j, masked)
        for e in heads:
            s = s_next
            if e + 1 < HEADS_PER_STEP:
                s_next = scores(e + 1, j, masked)
            finish(e, j, s)

    def run(finish):
        reset()
        all_heads(qi, True, two_pass_finish)

        def body(j, carry):
            all_heads(j, False, finish)
            return carry

        lax.fori_loop(0, qi, body, 0)
        write_out()

    run(one_pass_finish)
    worst_gap = functools.reduce(jnp.maximum, [jnp.max(gap_scr[e]) for e in heads])

    @pl.when(worst_gap > GAP_LIMIT)
    def _():
        run(two_pass_finish)


def _mla_attn(qt, k, vt):
    b, _, s = qt.shape
    t = ATTN_TILE
    nt = s // t
    groups = MLA_HEADS // HEADS_PER_STEP
    return pl.pallas_call(
        functools.partial(_attn_kernel, t=t),
        grid=(b, groups, nt),
        in_specs=[
            pl.BlockSpec((1, HEADS_PER_STEP * HEAD_SLAB, t), lambda bi, p, qi: (bi, p, qi)),
            pl.BlockSpec((1, s, HEADS_PER_STEP * HEAD_SLAB), lambda bi, p, qi: (bi, 0, p)),
            pl.BlockSpec((1, nt, HEADS_PER_STEP * V_HEAD_DIM, t), lambda bi, p, qi: (bi, 0, p, 0)),
        ],
        out_specs=pl.BlockSpec((1, t, HEADS_PER_STEP * V_HEAD_DIM), lambda bi, p, qi: (bi, qi, p)),
        out_shape=jax.ShapeDtypeStruct((b, s, MLA_HEADS * V_HEAD_DIM), BF16),
        scratch_shapes=[pltpu.VMEM((HEADS_PER_STEP, 1, t), F32),
                        pltpu.VMEM((HEADS_PER_STEP, V_HEAD_DIM + L_ROWS, t), F32),
                        pltpu.VMEM((HEADS_PER_STEP, 1, t), F32)],
        compiler_params=pltpu.CompilerParams(
            dimension_semantics=("arbitrary", "arbitrary", "arbitrary"), vmem_limit_bytes=VMEM_LIMIT),
        name="mla_attn",
    )(qt, k, vt)


def _gelu_tanh(x):
    return x * (0.5 * (1.0 + jnp.tanh(0.7978845608028654 * (x + 0.044715 * (x * x * x)))))


def _mixer_out_kernel(x_ref, attn_ref, c2_ref, p_ref, wo_ref, g_mpost_ref, g_fpre_ref, wg_ref, wu_ref,
                      dww_ref, dwb_ref, wd_ref, g_fpost_ref, wpp_ref, g_ple_ref, wpg_ref,
                      o_ref, gcarry_ref, *, tm, sub):
    subs = range(tm // sub)
    rows = [slice(h * sub, (h + 1) * sub) for h in subs]

    @pl.when(pl.program_id(1) == 0)
    def _():
        gcarry_ref[...] = jnp.zeros((FFN_HALO, D_FF), F32)

    mix = [jnp.dot(jnp.concatenate([attn_ref[0, r, :], c2_ref[0, r, :]], axis=-1), wo_ref[...],
                   preferred_element_type=F32) for r in rows]
    ple = [jnp.dot(p_ref[0, r, :].astype(BF16), wpp_ref[...], preferred_element_type=F32) for r in rows]
    h1 = [x_ref[0, rows[h], :] + _rms(mix[h], g_mpost_ref[...]) for h in subs]
    hn = [_rms(h1[h], g_fpre_ref[...]).astype(BF16) for h in subs]

    starts = list(range(0, D_FF, FFN_CHUNK))

    def gate_up(h, c0):
        cols = slice(c0, min(c0 + FFN_CHUNK, D_FF))
        return (jnp.dot(hn[h], wg_ref[:, cols], preferred_element_type=F32),
                jnp.dot(hn[h], wu_ref[:, cols], preferred_element_type=F32))

    def ffn_chunks(h, pending):
        ffn = jnp.zeros((sub, D_MODEL), F32)
        for n, c0 in enumerate(starts):
            cols = slice(c0, min(c0 + FFN_CHUNK, D_FF))
            gate, up = pending.pop(0)
            if n + FFN_AHEAD < len(starts):
                pending.append(gate_up(h, starts[n + FFN_AHEAD]))
            gext = jnp.concatenate([gcarry_ref[:, cols], gate], axis=0)
            g = (dwb_ref[:, cols] + dww_ref[0:1, cols] * gext[FFN_HALO - 2:FFN_HALO - 2 + sub]
                 + dww_ref[1:2, cols] * gext[FFN_HALO - 1:FFN_HALO - 1 + sub] + dww_ref[2:3, cols] * gate)
            gcarry_ref[:, cols] = gate[sub - FFN_HALO:]
            act = (_gelu_tanh(g) * up).astype(BF16)
            ffn = ffn + jnp.dot(act, wd_ref[cols, :], preferred_element_type=F32)
        return ffn

    pending = [gate_up(0, c0) for c0 in starts[:FFN_AHEAD]]
    for h in subs:
        ffn = ffn_chunks(h, pending)
        if h + 1 < len(subs):
            pending = [gate_up(h + 1, c0) for c0 in starts[:FFN_AHEAD]]
        h2 = h1[h] + _rms(ffn, g_fpost_ref[...])
        gt = jax.nn.sigmoid(jnp.dot(h2.astype(BF16), wpg_ref[...], preferred_element_type=F32))
        o_ref[0, rows[h], :] = h2 + gt * _rms(ple[h], g_ple_ref[...])


def _mixer_out(x, attn, c2, p, wo, g_mpost, g_fpre, wg, wu, dww, dwb, wd, g_fpost, wpp, g_ple, wpg):
    b, s, d = x.shape
    tm, sub = TM_OUT, SUB_OUT
    nt = s // tm
    tok = lambda w: pl.BlockSpec((1, tm, w), lambda bi, i: (bi, i, 0))
    return pl.pallas_call(
        functools.partial(_mixer_out_kernel, tm=tm, sub=sub),
        grid=(b, nt),
        in_specs=[
            tok(d), tok(MLA_HEADS * V_HEAD_DIM), tok(CONV_CHANNELS), tok(PLE_DIM),
            _const_spec(wo.shape), _const_spec((1, d)), _const_spec((1, d)),
            _const_spec(wg.shape), _const_spec(wu.shape), _const_spec(dww.shape), _const_spec((1, D_FF)),
            _const_spec(wd.shape), _const_spec((1, d)), _const_spec(wpp.shape), _const_spec((1, d)),
            _const_spec(wpg.shape),
        ],
        out_specs=tok(d),
        out_shape=jax.ShapeDtypeStruct((b, s, d), F32),
        scratch_shapes=[pltpu.VMEM((FFN_HALO, D_FF), F32)],
        compiler_params=pltpu.CompilerParams(
            dimension_semantics=("arbitrary", "arbitrary"), vmem_limit_bytes=VMEM_LIMIT),
        name="mixer_out",
    )(x, attn, c2, p, wo, g_mpost, g_fpre, wg, wu, dww, dwb, wd, g_fpost, wpp, g_ple, wpg)


def kernel(x, p, positions, g_mix_pre, w_in, g_q_a, w_q_b, g_kv_a, w_kv_b, conv_w, conv_b, conv_ln_g, conv_ln_b, w_o, g_mix_post, g_ffn_pre, w_ffn_gate, w_ffn_up, ffn_dw_w, ffn_dw_b, w_ffn_down, g_ffn_post, w_ple_proj, g_ple, w_ple_gate):
    b, s, d = x.shape
    assert w_in.shape[0] == 1 and (b, s, d) == (x.shape[0], x.shape[1], D_MODEL)
    assert s % TM_IN == 0 and s % TM_OUT == 0 and TM_IN % ATTN_TILE == 0 and ATTN_TILE % SUB_IN == 0
    row = lambda v: v[0][None, :].astype(F32)

    w_in0 = w_in[0]
    o_kr = Q_LORA_RANK + KV_LORA_RANK
    o_conv = o_kr + QK_ROPE_DIM
    w_kr = w_in0[:, o_kr:o_conv]
    w_kp = jnp.concatenate([-w_kr[:, HALF_ROPE:], w_kr[:, :HALF_ROPE]], axis=1)
    zc = lambda n: jnp.zeros((d, n), F32)
    slab = lambda w: jnp.concatenate([zc(QK_NOPE_DIM), w, zc(HEAD_SLAB - QK_DIM)], axis=1)
    w1 = jnp.concatenate([w_in0[:, :o_kr], w_in0[:, o_conv:], slab(w_kr), slab(w_kp)], axis=1).astype(BF16)

    wqt = w_q_b[0].T.astype(BF16)
    wkv = w_kv_b[0].astype(BF16)
    wvt = (w_kv_b[0].reshape(KV_LORA_RANK, MLA_HEADS, QK_NOPE_DIM + V_HEAD_DIM)[:, :, QK_NOPE_DIM:]
           .reshape(KV_LORA_RANK, MLA_HEADS * V_HEAD_DIM).T.astype(BF16))
    invf = (ROPE_THETA ** (-jnp.arange(0, QK_ROPE_DIM, 2, dtype=F32) / QK_ROPE_DIM))[:, None]
    pos3 = positions.reshape(b, 1, s)

    qt, k, vt, c2 = _mixer_in(pos3, x, row(g_mix_pre), w1, row(g_q_a), wqt, row(g_kv_a), wkv, wvt, invf,
                              conv_w[0].astype(F32), row(conv_b), row(conv_ln_g), row(conv_ln_b))
    attn = _mla_attn(qt, k, vt)
    return _mixer_out(x, attn, c2, p[0], w_o[0].astype(BF16), row(g_mix_post), row(g_ffn_pre),
                      w_ffn_gate[0].astype(BF16), w_ffn_up[0].astype(BF16), ffn_dw_w[0].astype(F32),
                      row(ffn_dw_b), w_ffn_down[0].astype(BF16), row(g_ffn_post),
                      w_ple_proj[0].astype(BF16), row(g_ple), w_ple_gate[0].astype(BF16))
```

```python
import functools

import jax
import jax.numpy as jnp
from jax import lax
from jax.experimental import pallas as pl
from jax.experimental.pallas import tpu as pltpu

D_MODEL = 1024
PLE_DIM = 256
MLA_HEADS = 8
QK_NOPE_DIM = 64
QK_ROPE_DIM = 32
QK_DIM = QK_NOPE_DIM + QK_ROPE_DIM
V_HEAD_DIM = 64
Q_LORA_RANK = 384
KV_LORA_RANK = 256
ROPE_THETA = 10000.0
CONV_CHANNELS = 512
CONV_WIDTH = 31
D_FF = 2816
FFN_CONV_WIDTH = 3
NORM_EPS = 1e-6

LANES = 128
SUBLANES = 8
FFN_CHUNK = 256
FFN_AHEAD = 3
HEAD_SLAB = LANES
SHIFT_ROWS = 16
HALF_ROPE = QK_ROPE_DIM // 2
CONV_HALO = 32
FFN_HALO = 8
SCORE_SCALE = (QK_DIM ** -0.5) * 1.4426950408889634

TM_IN = 1024
ATTN_TILE = 512
SUB_IN = 256
TM_OUT = 512
SUB_OUT = 256
VMEM_LIMIT = 56 * 1024 * 1024

BF16 = jnp.bfloat16
F32 = jnp.float32
NT_DIMS = (((1,), (1,)), ((), ()))


def _rms(x, g):
    return x * lax.rsqrt(jnp.mean(x * x, axis=-1, keepdims=True) + NORM_EPS) * g


def _const_spec(shape):
    nd = len(shape)
    return pl.BlockSpec(shape, lambda *_: (0,) * nd, pipeline_mode=pl.Buffered(1))


def _conformer_conv(c, h, sub, convw_ref, convb_ref, lng_ref, lnb_ref, cext_ref, shift_ref):
    lo = h * sub
    cext_ref[CONV_HALO + lo:CONV_HALO + lo + sub, :] = c
    span = sub + CONV_HALO - SUBLANES
    for r in range(1, SUBLANES):
        shift_ref[r - 1, h % 2, 0:span, :] = cext_ref[lo + r:lo + r + span, :]
    acc = jnp.broadcast_to(convb_ref[...], (sub, CONV_CHANNELS))
    for j in range(CONV_WIDTH):
        a, r = divmod(CONV_HALO - (CONV_WIDTH - 1) + j, SUBLANES)
        if r == 0:
            tap = cext_ref[lo + a * SUBLANES:lo + a * SUBLANES + sub, :]
        else:
            tap = shift_ref[r - 1, h % 2, a * SUBLANES:a * SUBLANES + sub, :]
        acc = acc + convw_ref[j:j + 1, :] * tap
    mu = jnp.mean(acc, axis=-1, keepdims=True)
    xc = acc - mu
    y = xc * lax.rsqrt(jnp.mean(xc * xc, axis=-1, keepdims=True) + NORM_EPS) * lng_ref[...] + lnb_ref[...]
    return (y * jax.nn.sigmoid(y)).astype(BF16)


def _mixer_in_kernel(pos_ref, x_ref, g_pre_ref, w1_ref, gq_ref, wqt_ref, gkv_ref, wkv_ref, wvt_ref,
                     invf_ref, convw_ref, convb_ref, lng_ref, lnb_ref,
                     qt_ref, k_ref, vt_ref, c2_ref, cext_ref, shift_ref, *, tm, sub):
    o_kv = Q_LORA_RANK
    o_a = o_kv + KV_LORA_RANK
    o_g = o_a + CONV_CHANNELS
    o_kr = o_g + CONV_CHANNELS
    o_kp = o_kr + LANES

    @pl.when(pl.program_id(1) == 0)
    def _():
        cext_ref[0:CONV_HALO, :] = jnp.zeros((CONV_HALO, CONV_CHANNELS), F32)

    for h in range(tm // sub):
        rows = slice(h * sub, (h + 1) * sub)
        xn = _rms(x_ref[0, rows, :], g_pre_ref[...]).astype(BF16)
        proj = jnp.dot(xn, w1_ref[...], preferred_element_type=F32)

        c = proj[:, o_a:o_g] * jax.nn.sigmoid(proj[:, o_g:o_kr])
        c2_ref[0, rows, :] = _conformer_conv(c, h, sub, convw_ref, convb_ref, lng_ref, lnb_ref,
                                             cext_ref, shift_ref)

        qn = _rms(proj[:, :o_kv], gq_ref[...]).astype(BF16)
        kvn = _rms(proj[:, o_kv:o_a], gkv_ref[...]).astype(BF16)

        ang = invf_ref[...] * pos_ref[0, :, rows].astype(F32)
        cos_t = jnp.cos(ang)
        sin_t = jnp.sin(ang)

        qt = lax.dot_general(wqt_ref[...], qn, NT_DIMS, preferred_element_type=F32) * SCORE_SCALE
        zero_rows = jnp.zeros((HEAD_SLAB - QK_DIM, sub), BF16)
        for hd in range(MLA_HEADS):
            r0 = hd * QK_DIM
            o0 = hd * HEAD_SLAB
            x1 = qt[r0 + QK_NOPE_DIM:r0 + QK_NOPE_DIM + HALF_ROPE]
            x2 = qt[r0 + QK_NOPE_DIM + HALF_ROPE:r0 + QK_DIM]
            qt_ref[0, o0:o0 + QK_NOPE_DIM, rows] = qt[r0:r0 + QK_NOPE_DIM].astype(BF16)
            qt_ref[0, o0 + QK_NOPE_DIM:o0 + QK_NOPE_DIM + HALF_ROPE, rows] = (x1 * cos_t - x2 * sin_t).astype(BF16)
            qt_ref[0, o0 + QK_NOPE_DIM + HALF_ROPE:o0 + QK_DIM, rows] = (x1 * sin_t + x2 * cos_t).astype(BF16)
            qt_ref[0, o0 + QK_DIM:o0 + HEAD_SLAB, rows] = zero_rows

        pad_lo = jnp.zeros((QK_NOPE_DIM, sub), F32)
        pad_hi = jnp.zeros((HEAD_SLAB - QK_DIM, sub), F32)
        cos_tok = jnp.concatenate([pad_lo, cos_t, cos_t, pad_hi], axis=0).T
        sin_tok = jnp.concatenate([pad_lo, sin_t, sin_t, pad_hi], axis=0).T
        k_rope = proj[:, o_kr:o_kp] * cos_tok + proj[:, o_kp:o_kp + LANES] * sin_tok

        kfull = jnp.dot(kvn, wkv_ref[...], preferred_element_type=F32)
        lane = lax.broadcasted_iota(jnp.int32, (sub, HEAD_SLAB), 1)
        k_rope = jnp.where((lane >= QK_DIM) & (lane < QK_DIM + SHIFT_ROWS), 1.0, k_rope)
        for hd in range(MLA_HEADS):
            slab = kfull[:, hd * HEAD_SLAB:(hd + 1) * HEAD_SLAB]
            k_ref[0, rows, hd * HEAD_SLAB:(hd + 1) * HEAD_SLAB] = (
                jnp.where(lane < QK_NOPE_DIM, slab, k_rope).astype(BF16))

        vt = lax.dot_general(wvt_ref[...], kvn, NT_DIMS, preferred_element_type=F32)
        per_kv = ATTN_TILE // sub
        vt_ref[0, h // per_kv, :, (h % per_kv) * sub:(h % per_kv + 1) * sub] = vt.astype(BF16)

    cext_ref[0:CONV_HALO, :] = cext_ref[tm:tm + CONV_HALO, :]


def _mixer_in(pos3, x, g_pre, w1, gq, wqt, gkv, wkv, wvt, invf, convw, convb, lng, lnb):
    b, s, d = x.shape
    tm, sub = TM_IN, SUB_IN
    nt = s // tm
    n1 = w1.shape[1]
    chan = _const_spec((1, CONV_CHANNELS))
    return pl.pallas_call(
        functools.partial(_mixer_in_kernel, tm=tm, sub=sub),
        grid=(b, nt),
        in_specs=[
            pl.BlockSpec((1, 1, tm), lambda bi, i: (bi, 0, i)),
            pl.BlockSpec((1, tm, d), lambda bi, i: (bi, i, 0)),
            _const_spec((1, d)),
            _const_spec((d, n1)),
            _const_spec((1, Q_LORA_RANK)),
            _const_spec(wqt.shape),
            _const_spec((1, KV_LORA_RANK)),
            _const_spec(wkv.shape),
            _const_spec(wvt.shape),
            _const_spec(invf.shape),
            _const_spec(convw.shape), chan, chan, chan,
        ],
        out_specs=[
            pl.BlockSpec((1, MLA_HEADS * HEAD_SLAB, tm), lambda bi, i: (bi, 0, i)),
            pl.BlockSpec((1, tm, MLA_HEADS * HEAD_SLAB), lambda bi, i: (bi, i, 0)),
            pl.BlockSpec((1, tm // ATTN_TILE, MLA_HEADS * V_HEAD_DIM, ATTN_TILE), lambda bi, i: (bi, i, 0, 0)),
            pl.BlockSpec((1, tm, CONV_CHANNELS), lambda bi, i: (bi, i, 0)),
        ],
        out_shape=[
            jax.ShapeDtypeStruct((b, MLA_HEADS * HEAD_SLAB, s), BF16),
            jax.ShapeDtypeStruct((b, s, MLA_HEADS * HEAD_SLAB), BF16),
            jax.ShapeDtypeStruct((b, s // ATTN_TILE, MLA_HEADS * V_HEAD_DIM, ATTN_TILE), BF16),
            jax.ShapeDtypeStruct((b, s, CONV_CHANNELS), BF16),
        ],
        scratch_shapes=[pltpu.VMEM((CONV_HALO + tm, CONV_CHANNELS), F32),
                        pltpu.VMEM((SUBLANES - 1, 2, CONV_HALO + sub, CONV_CHANNELS), F32)],
        compiler_params=pltpu.CompilerParams(
            dimension_semantics=("arbitrary", "arbitrary"), vmem_limit_bytes=VMEM_LIMIT),
        name="mixer_in",
    )(pos3, x, g_pre, w1, gq, wqt, gkv, wkv, wvt, invf, convw, convb, lng, lnb)


HEADS_PER_STEP = 8
L_ROWS = 16
GAP_LIMIT = 16.0


def _attn_kernel(qt_ref, k_ref, vt_ref, o_ref, q_scr, m_scr, acc_scr, gap_scr, *, t):
    qi = pl.program_id(2)
    heads = range(HEADS_PER_STEP)
    ones = jnp.ones((L_ROWS, t), BF16)

    def load_q():
        for e in heads:
            q_scr[e] = qt_ref[0, e * HEAD_SLAB:(e + 1) * HEAD_SLAB, :]

    def set_shift(e, m):
        rows = jnp.broadcast_to(m * (-1.0 / SHIFT_ROWS), (SHIFT_ROWS, t))
        q_scr[e, QK_DIM:QK_DIM + SHIFT_ROWS, :] = rows.astype(BF16)

    def bf16_value(x):
        return x.astype(BF16).astype(F32)

    def scores(e, j, masked):
        kj = k_ref[0, pl.ds(pl.multiple_of(j * t, t), t), e * HEAD_SLAB:(e + 1) * HEAD_SLAB]
        s = jnp.dot(kj, q_scr[e], preferred_element_type=F32)
        if masked:
            row = lax.broadcasted_iota(jnp.int32, (t, t), 0)
            col = lax.broadcasted_iota(jnp.int32, (t, t), 1)
            s = jnp.where(row <= col, s, -jnp.inf)
        return s

    def v_aug(e, j):
        return jnp.concatenate([vt_ref[0, j, e * V_HEAD_DIM:(e + 1) * V_HEAD_DIM, :], ones], axis=0)

    def first_finish(e, j, s):
        m = bf16_value(jnp.max(s, axis=0, keepdims=True))
        p = jnp.exp2(s - m).astype(BF16)
        acc_scr[e] = jnp.dot(v_aug(e, j), p, preferred_element_type=F32)
        m_scr[e] = m
        gap_scr[e] = jnp.zeros((1, t), F32)
        set_shift(e, m)

    def one_pass_finish(e, j, s):
        m = m_scr[e]
        p = jnp.exp2(s).astype(BF16)
        cm = jnp.max(s, axis=0, keepdims=True)
        m_new = bf16_value(m + jnp.maximum(cm, 0.0))
        gap_scr[e] = jnp.maximum(gap_scr[e], cm)
        acc_scr[e] = jnp.exp2(m - m_new) * (acc_scr[e] + jnp.dot(v_aug(e, j), p, preferred_element_type=F32))
        m_scr[e] = m_new
        set_shift(e, m_new)

    def two_pass_finish(e, j, s):
        m = m_scr[e]
        m_new = jnp.maximum(m, jnp.max(s, axis=0, keepdims=True))
        p = jnp.exp2(s - m_new).astype(BF16)
        acc_scr[e] = jnp.exp2(m - m_new) * acc_scr[e] + jnp.dot(v_aug(e, j), p, preferred_element_type=F32)
        m_scr[e] = m_new

    def write_out():
        outs = [acc_scr[e, :V_HEAD_DIM, :] * (1.0 / acc_scr[e, V_HEAD_DIM:V_HEAD_DIM + 1, :]) for e in heads]
        o_ref[0] = jnp.concatenate(outs, axis=0).T.astype(BF16)

    def all_heads(j, masked, finish):
        s_next = scores(0, j, masked)
        for e in heads:
            s = s_next
            if e + 1 < HEADS_PER_STEP:
                s_next = scores(e + 1, j, masked)
            finish(e, j, s)

    def run(first, finish):
        load_q()
        all_heads(qi, True, first)

        def body(j, carry):
            all_heads(j, False, finish)
            return carry

        lax.fori_loop(0, qi, body, 0)
        write_out()

    run(first_finish, one_pass_finish)
    worst_gap = functools.reduce(jnp.maximum, [jnp.max(gap_scr[e]) for e in heads])

    @pl.when(worst_gap > GAP_LIMIT)
    def _():
        for e in heads:
            m_scr[e] = jnp.full((1, t), -jnp.inf, F32)
            acc_scr[e] = jnp.zeros((V_HEAD_DIM + L_ROWS, t), F32)
        run(two_pass_finish, two_pass_finish)


def _mla_attn(qt, k, vt):
    b, _, s = qt.shape
    t = ATTN_TILE
    nt = s // t
    groups = MLA_HEADS // HEADS_PER_STEP
    return pl.pallas_call(
        functools.partial(_attn_kernel, t=t),
        grid=(b, groups, nt),
        in_specs=[
            pl.BlockSpec((1, HEADS_PER_STEP * HEAD_SLAB, t), lambda bi, p, qi: (bi, p, qi)),
            pl.BlockSpec((1, s, HEADS_PER_STEP * HEAD_SLAB), lambda bi, p, qi: (bi, 0, p)),
            pl.BlockSpec((1, nt, HEADS_PER_STEP * V_HEAD_DIM, t), lambda bi, p, qi: (bi, 0, p, 0)),
        ],
        out_specs=pl.BlockSpec((1, t, HEADS_PER_STEP * V_HEAD_DIM), lambda bi, p, qi: (bi, qi, p)),
        out_shape=jax.ShapeDtypeStruct((b, s, MLA_HEADS * V_HEAD_DIM), BF16),
        scratch_shapes=[pltpu.VMEM((HEADS_PER_STEP, HEAD_SLAB, t), BF16),
                        pltpu.VMEM((HEADS_PER_STEP, 1, t), F32),
                        pltpu.VMEM((HEADS_PER_STEP, V_HEAD_DIM + L_ROWS, t), F32),
                        pltpu.VMEM((HEADS_PER_STEP, 1, t), F32)],
        compiler_params=pltpu.CompilerParams(
            dimension_semantics=("arbitrary", "arbitrary", "arbitrary"), vmem_limit_bytes=VMEM_LIMIT),
        name="mla_attn",
    )(qt, k, vt)


def _gelu_tanh(x):
    return x * (0.5 * (1.0 + jnp.tanh(0.7978845608028654 * (x + 0.044715 * (x * x * x)))))


def _mixer_out_kernel(x_ref, attn_ref, c2_ref, p_ref, wo_ref, g_mpost_ref, g_fpre_ref, wg_ref, wu_ref,
                      dww_ref, dwb_ref, wd_ref, g_fpost_ref, wpp_ref, g_ple_ref, wpg_ref,
                      o_ref, gcarry_ref, *, tm, sub):
    subs = range(tm // sub)
    rows = [slice(h * sub, (h + 1) * sub) for h in subs]

    @pl.when(pl.program_id(1) == 0)
    def _():
        gcarry_ref[...] = jnp.zeros((FFN_HALO, D_FF), F32)

    mix = [jnp.dot(jnp.concatenate([attn_ref[0, r, :], c2_ref[0, r, :]], axis=-1), wo_ref[...],
                   preferred_element_type=F32) for r in rows]
    ple = [jnp.dot(p_ref[0, r, :].astype(BF16), wpp_ref[...], preferred_element_type=F32) for r in rows]
    h1 = [x_ref[0, rows[h], :] + _rms(mix[h], g_mpost_ref[...]) for h in subs]
    hn = [_rms(h1[h], g_fpre_ref[...]).astype(BF16) for h in subs]

    starts = list(range(0, D_FF, FFN_CHUNK))

    def gate_up(h, c0):
        cols = slice(c0, min(c0 + FFN_CHUNK, D_FF))
        return (jnp.dot(hn[h], wg_ref[:, cols], preferred_element_type=F32),
                jnp.dot(hn[h], wu_ref[:, cols], preferred_element_type=F32))

    def ffn_chunks(h, pending):
        ffn = jnp.zeros((sub, D_MODEL), F32)
        for n, c0 in enumerate(starts):
            cols = slice(c0, min(c0 + FFN_CHUNK, D_FF))
            gate, up = pending.pop(0)
            if n + FFN_AHEAD < len(starts):
                pending.append(gate_up(h, starts[n + FFN_AHEAD]))
            gext = jnp.concatenate([gcarry_ref[:, cols], gate], axis=0)
            g = (dwb_ref[:, cols] + dww_ref[0:1, cols] * gext[FFN_HALO - 2:FFN_HALO - 2 + sub]
                 + dww_ref[1:2, cols] * gext[FFN_HALO - 1:FFN_HALO - 1 + sub] + dww_ref[2:3, cols] * gate)
            gcarry_ref[:, cols] = gate[sub - FFN_HALO:]
            act = (_gelu_tanh(g) * up).astype(BF16)
            ffn = ffn + jnp.dot(act, wd_ref[cols, :], preferred_element_type=F32)
        return ffn

    pending = [gate_up(0, c0) for c0 in starts[:FFN_AHEAD]]
    for h in subs:
        ffn = ffn_chunks(h, pending)
        if h + 1 < len(subs):
            pending = [gate_up(h + 1, c0) for c0 in starts[:FFN_AHEAD]]
        h2 = h1[h] + _rms(ffn, g_fpost_ref[...])
        gt = jax.nn.sigmoid(jnp.dot(h2.astype(BF16), wpg_ref[...], preferred_element_type=F32))
        o_ref[0, rows[h], :] = h2 + gt * _rms(ple[h], g_ple_ref[...])


def _mixer_out(x, attn, c2, p, wo, g_mpost, g_fpre, wg, wu, dww, dwb, wd, g_fpost, wpp, g_ple, wpg):
    b, s, d = x.shape
    tm, sub = TM_OUT, SUB_OUT
    nt = s // tm
    tok = lambda w: pl.BlockSpec((1, tm, w), lambda bi, i: (bi, i, 0))
    return pl.pallas_call(
        functools.partial(_mixer_out_kernel, tm=tm, sub=sub),
        grid=(b, nt),
        in_specs=[
            tok(d), tok(MLA_HEADS * V_HEAD_DIM), tok(CONV_CHANNELS), tok(PLE_DIM),
            _const_spec(wo.shape), _const_spec((1, d)), _const_spec((1, d)),
            _const_spec(wg.shape), _const_spec(wu.shape), _const_spec(dww.shape), _const_spec((1, D_FF)),
            _const_spec(wd.shape), _const_spec((1, d)), _const_spec(wpp.shape), _const_spec((1, d)),
            _const_spec(wpg.shape),
        ],
        out_specs=tok(d),
        out_shape=jax.ShapeDtypeStruct((b, s, d), F32),
        scratch_shapes=[pltpu.VMEM((FFN_HALO, D_FF), F32)],
        compiler_params=pltpu.CompilerParams(
            dimension_semantics=("arbitrary", "arbitrary"), vmem_limit_bytes=VMEM_LIMIT),
        name="mixer_out",
    )(x, attn, c2, p, wo, g_mpost, g_fpre, wg, wu, dww, dwb, wd, g_fpost, wpp, g_ple, wpg)


def kernel(x, p, positions, g_mix_pre, w_in, g_q_a, w_q_b, g_kv_a, w_kv_b, conv_w, conv_b, conv_ln_g, conv_ln_b, w_o, g_mix_post, g_ffn_pre, w_ffn_gate, w_ffn_up, ffn_dw_w, ffn_dw_b, w_ffn_down, g_ffn_post, w_ple_proj, g_ple, w_ple_gate):
    b, s, d = x.shape
    assert w_in.shape[0] == 1 and (b, s, d) == (x.shape[0], x.shape[1], D_MODEL)
    assert s % TM_IN == 0 and s % TM_OUT == 0 and TM_IN % ATTN_TILE == 0 and ATTN_TILE % SUB_IN == 0
    row = lambda v: v[0][None, :].astype(F32)

    w_in0 = w_in[0]
    o_kr = Q_LORA_RANK + KV_LORA_RANK
    o_conv = o_kr + QK_ROPE_DIM
    w_kr = w_in0[:, o_kr:o_conv]
    w_kp = jnp.concatenate([-w_kr[:, HALF_ROPE:], w_kr[:, :HALF_ROPE]], axis=1)
    zc = lambda n: jnp.zeros((d, n), F32)
    slab = lambda w: jnp.concatenate([zc(QK_NOPE_DIM), w, zc(HEAD_SLAB - QK_DIM)], axis=1)
    w1 = jnp.concatenate([w_in0[:, :o_kr], w_in0[:, o_conv:], slab(w_kr), slab(w_kp)], axis=1).astype(BF16)

    wqt = w_q_b[0].T.astype(BF16)
    wkv = w_kv_b[0].astype(BF16)
    wvt = (w_kv_b[0].reshape(KV_LORA_RANK, MLA_HEADS, QK_NOPE_DIM + V_HEAD_DIM)[:, :, QK_NOPE_DIM:]
           .reshape(KV_LORA_RANK, MLA_HEADS * V_HEAD_DIM).T.astype(BF16))
    invf = (ROPE_THETA ** (-jnp.arange(0, QK_ROPE_DIM, 2, dtype=F32) / QK_ROPE_DIM))[:, None]
    pos3 = positions.reshape(b, 1, s)

    qt, k, vt, c2 = _mixer_in(pos3, x, row(g_mix_pre), w1, row(g_q_a), wqt, row(g_kv_a), wkv, wvt, invf,
                              conv_w[0].astype(F32), row(conv_b), row(conv_ln_g), row(conv_ln_b))
    attn = _mla_attn(qt, k, vt)
    return _mixer_out(x, attn, c2, p[0], w_o[0].astype(BF16), row(g_mix_post), row(g_ffn_pre),
                      w_ffn_gate[0].astype(BF16), w_ffn_up[0].astype(BF16), ffn_dw_w[0].astype(F32),
                      row(ffn_dw_b), w_ffn_down[0].astype(BF16), row(g_ffn_post),
                      w_ple_proj[0].astype(BF16), row(g_ple), w_ple_gate[0].astype(BF16))
```

```python
import functools

import jax
import jax.numpy as jnp
from jax import lax
from jax.experimental import pallas as pl
from jax.experimental.pallas import tpu as pltpu

D_MODEL = 1024
PLE_DIM = 256
MLA_HEADS = 8
QK_NOPE_DIM = 64
QK_ROPE_DIM = 32
QK_DIM = QK_NOPE_DIM + QK_ROPE_DIM
V_HEAD_DIM = 64
Q_LORA_RANK = 384
KV_LORA_RANK = 256
ROPE_THETA = 10000.0
CONV_CHANNELS = 512
CONV_WIDTH = 31
D_FF = 2816
FFN_CONV_WIDTH = 3
NORM_EPS = 1e-6

LANES = 128
SUBLANES = 8
FFN_CHUNK = 256
FFN_AHEAD = 3
HEAD_SLAB = LANES
SHIFT_ROWS = 16
HALF_ROPE = QK_ROPE_DIM // 2
CONV_HALO = 32
FFN_HALO = 8
SCORE_SCALE = (QK_DIM ** -0.5) * 1.4426950408889634

TM_IN = 1024
ATTN_TILE = 512
SUB_IN = 256
TM_OUT = 512
SUB_OUT = 256
VMEM_LIMIT = 56 * 1024 * 1024

BF16 = jnp.bfloat16
F32 = jnp.float32
NT_DIMS = (((1,), (1,)), ((), ()))


def _rms(x, g):
    return x * lax.rsqrt(jnp.mean(x * x, axis=-1, keepdims=True) + NORM_EPS) * g


def _const_spec(shape):
    nd = len(shape)
    return pl.BlockSpec(shape, lambda *_: (0,) * nd, pipeline_mode=pl.Buffered(1))


def _conformer_conv(c, h, sub, convw_ref, convb_ref, lng_ref, lnb_ref, cext_ref, shift_ref):
    lo = h * sub
    cext_ref[CONV_HALO + lo:CONV_HALO + lo + sub, :] = c
    span = sub + CONV_HALO - SUBLANES
    for r in range(1, SUBLANES):
        shift_ref[r - 1, h % 2, 0:span, :] = cext_ref[lo + r:lo + r + span, :]
    acc = jnp.broadcast_to(convb_ref[...], (sub, CONV_CHANNELS))
    for j in range(CONV_WIDTH):
        a, r = divmod(CONV_HALO - (CONV_WIDTH - 1) + j, SUBLANES)
        if r == 0:
            tap = cext_ref[lo + a * SUBLANES:lo + a * SUBLANES + sub, :]
        else:
            tap = shift_ref[r - 1, h % 2, a * SUBLANES:a * SUBLANES + sub, :]
        acc = acc + convw_ref[j:j + 1, :] * tap
    mu = jnp.mean(acc, axis=-1, keepdims=True)
    xc = acc - mu
    y = xc * lax.rsqrt(jnp.mean(xc * xc, axis=-1, keepdims=True) + NORM_EPS) * lng_ref[...] + lnb_ref[...]
    return (y * jax.nn.sigmoid(y)).astype(BF16)


def _mixer_in_kernel(pos_ref, x_ref, g_pre_ref, w1_ref, gq_ref, wqt_ref, gkv_ref, wkv_ref, wvt_ref,
                     invf_ref, convw_ref, convb_ref, lng_ref, lnb_ref,
                     qt_ref, k_ref, vt_ref, c2_ref, cext_ref, shift_ref, *, tm, sub):
    o_kv = Q_LORA_RANK
    o_a = o_kv + KV_LORA_RANK
    o_g = o_a + CONV_CHANNELS
    o_kr = o_g + CONV_CHANNELS
    o_kp = o_kr + LANES

    @pl.when(pl.program_id(1) == 0)
    def _():
        cext_ref[0:CONV_HALO, :] = jnp.zeros((CONV_HALO, CONV_CHANNELS), F32)

    for h in range(tm // sub):
        rows = slice(h * sub, (h + 1) * sub)
        xn = _rms(x_ref[0, rows, :], g_pre_ref[...]).astype(BF16)
        proj = jnp.dot(xn, w1_ref[...], preferred_element_type=F32)

        c = proj[:, o_a:o_g] * jax.nn.sigmoid(proj[:, o_g:o_kr])
        c2_ref[0, rows, :] = _conformer_conv(c, h, sub, convw_ref, convb_ref, lng_ref, lnb_ref,
                                             cext_ref, shift_ref)

        qn = _rms(proj[:, :o_kv], gq_ref[...]).astype(BF16)
        kvn = _rms(proj[:, o_kv:o_a], gkv_ref[...]).astype(BF16)

        ang = invf_ref[...] * pos_ref[0, :, rows].astype(F32)
        cos_t = jnp.cos(ang)
        sin_t = jnp.sin(ang)

        qt = lax.dot_general(wqt_ref[...], qn, NT_DIMS, preferred_element_type=F32) * SCORE_SCALE
        zero_rows = jnp.zeros((HEAD_SLAB - QK_DIM, sub), BF16)
        for hd in range(MLA_HEADS):
            r0 = hd * QK_DIM
            o0 = hd * HEAD_SLAB
            x1 = qt[r0 + QK_NOPE_DIM:r0 + QK_NOPE_DIM + HALF_ROPE]
            x2 = qt[r0 + QK_NOPE_DIM + HALF_ROPE:r0 + QK_DIM]
            qt_ref[0, o0:o0 + QK_NOPE_DIM, rows] = qt[r0:r0 + QK_NOPE_DIM].astype(BF16)
            qt_ref[0, o0 + QK_NOPE_DIM:o0 + QK_NOPE_DIM + HALF_ROPE, rows] = (x1 * cos_t - x2 * sin_t).astype(BF16)
            qt_ref[0, o0 + QK_NOPE_DIM + HALF_ROPE:o0 + QK_DIM, rows] = (x1 * sin_t + x2 * cos_t).astype(BF16)
            qt_ref[0, o0 + QK_DIM:o0 + HEAD_SLAB, rows] = zero_rows

        pad_lo = jnp.zeros((QK_NOPE_DIM, sub), F32)
        pad_hi = jnp.zeros((HEAD_SLAB - QK_DIM, sub), F32)
        cos_tok = jnp.concatenate([pad_lo, cos_t, cos_t, pad_hi], axis=0).T
        sin_tok = jnp.concatenate([pad_lo, sin_t, sin_t, pad_hi], axis=0).T
        k_rope = proj[:, o_kr:o_kp] * cos_tok + proj[:, o_kp:o_kp + LANES] * sin_tok

        kfull = jnp.dot(kvn, wkv_ref[...], preferred_element_type=F32)
        lane = lax.broadcasted_iota(jnp.int32, (sub, HEAD_SLAB), 1)
        k_rope = jnp.where((lane >= QK_DIM) & (lane < QK_DIM + SHIFT_ROWS), 1.0, k_rope)
        for hd in range(MLA_HEADS):
            slab = kfull[:, hd * HEAD_SLAB:(hd + 1) * HEAD_SLAB]
            k_ref[0, rows, hd * HEAD_SLAB:(hd + 1) * HEAD_SLAB] = (
                jnp.where(lane < QK_NOPE_DIM, slab, k_rope).astype(BF16))

        vt = lax.dot_general(wvt_ref[...], kvn, NT_DIMS, preferred_element_type=F32)
        per_kv = ATTN_TILE // sub
        vt_ref[0, h // per_kv, :, (h % per_kv) * sub:(h % per_kv + 1) * sub] = vt.astype(BF16)

    cext_ref[0:CONV_HALO, :] = cext_ref[tm:tm + CONV_HALO, :]


def _mixer_in(pos3, x, g_pre, w1, gq, wqt, gkv, wkv, wvt, invf, convw, convb, lng, lnb):
    b, s, d = x.shape
    tm, sub = TM_IN, SUB_IN
    nt = s // tm
    n1 = w1.shape[1]
    chan = _const_spec((1, CONV_CHANNELS))
    return pl.pallas_call(
        functools.partial(_mixer_in_kernel, tm=tm, sub=sub),
        grid=(b, nt),
        in_specs=[
            pl.BlockSpec((1, 1, tm), lambda bi, i: (bi, 0, i)),
            pl.BlockSpec((1, tm, d), lambda bi, i: (bi, i, 0)),
            _const_spec((1, d)),
            _const_spec((d, n1)),
            _const_spec((1, Q_LORA_RANK)),
            _const_spec(wqt.shape),
            _const_spec((1, KV_LORA_RANK)),
            _const_spec(wkv.shape),
            _const_spec(wvt.shape),
            _const_spec(invf.shape),
            _const_spec(convw.shape), chan, chan, chan,
        ],
        out_specs=[
            pl.BlockSpec((1, MLA_HEADS * HEAD_SLAB, tm), lambda bi, i: (bi, 0, i)),
            pl.BlockSpec((1, tm, MLA_HEADS * HEAD_SLAB), lambda bi, i: (bi, i, 0)),
            pl.BlockSpec((1, tm // ATTN_TILE, MLA_HEADS * V_HEAD_DIM, ATTN_TILE), lambda bi, i: (bi, i, 0, 0)),
            pl.BlockSpec((1, tm, CONV_CHANNELS), lambda bi, i: (bi, i, 0)),
        ],
        out_shape=[
            jax.ShapeDtypeStruct((b, MLA_HEADS * HEAD_SLAB, s), BF16),
            jax.ShapeDtypeStruct((b, s, MLA_HEADS * HEAD_SLAB), BF16),
            jax.ShapeDtypeStruct((b, s // ATTN_TILE, MLA_HEADS * V_HEAD_DIM, ATTN_TILE), BF16),
            jax.ShapeDtypeStruct((b, s, CONV_CHANNELS), BF16),
        ],
        scratch_shapes=[pltpu.VMEM((CONV_HALO + tm, CONV_CHANNELS), F32),
                        pltpu.VMEM((SUBLANES - 1, 2, CONV_HALO + sub, CONV_CHANNELS), F32)],
        compiler_params=pltpu.CompilerParams(
            dimension_semantics=("arbitrary", "arbitrary"), vmem_limit_bytes=VMEM_LIMIT),
        name="mixer_in",
    )(pos3, x, g_pre, w1, gq, wqt, gkv, wkv, wvt, invf, convw, convb, lng, lnb)


HEADS_PER_STEP = 8
SCORES_AHEAD = 2
L_ROWS = 16
SUM_LIMIT = 2.0 ** 20


def _attn_kernel(qt_ref, k_ref, vt_ref, o_ref, q_scr, m_scr, acc_scr, *, t):
    qi = pl.program_id(2)
    heads = range(HEADS_PER_STEP)
    ones = jnp.ones((L_ROWS, t), BF16)

    def load_q():
        for e in heads:
            q_scr[e] = qt_ref[0, e * HEAD_SLAB:(e + 1) * HEAD_SLAB, :]

    def set_shift(e, m):
        rows = jnp.broadcast_to(m * (-1.0 / SHIFT_ROWS), (SHIFT_ROWS, t))
        q_scr[e, QK_DIM:QK_DIM + SHIFT_ROWS, :] = rows.astype(BF16)

    def bf16_value(x):
        return x.astype(BF16).astype(F32)

    def scores(e, j, masked):
        kj = k_ref[0, pl.ds(pl.multiple_of(j * t, t), t), e * HEAD_SLAB:(e + 1) * HEAD_SLAB]
        s = jnp.dot(kj, q_scr[e], preferred_element_type=F32)
        if masked:
            row = lax.broadcasted_iota(jnp.int32, (t, t), 0)
            col = lax.broadcasted_iota(jnp.int32, (t, t), 1)
            s = jnp.where(row <= col, s, -jnp.inf)
        return s

    def v_aug(e, j):
        return jnp.concatenate([vt_ref[0, j, e * V_HEAD_DIM:(e + 1) * V_HEAD_DIM, :], ones], axis=0)

    def first_finish(e, j, s):
        m = bf16_value(jnp.max(s, axis=0, keepdims=True))
        p = jnp.exp2(s - m).astype(BF16)
        acc_scr[e] = jnp.dot(v_aug(e, j), p, preferred_element_type=F32)
        set_shift(e, m)

    def one_pass_finish(e, j, s):
        acc_scr[e] += jnp.dot(v_aug(e, j), jnp.exp2(s).astype(BF16), preferred_element_type=F32)

    def two_pass_finish(e, j, s):
        m = m_scr[e]
        m_new = jnp.maximum(m, jnp.max(s, axis=0, keepdims=True))
        p = jnp.exp2(s - m_new).astype(BF16)
        acc_scr[e] = jnp.exp2(m - m_new) * acc_scr[e] + jnp.dot(v_aug(e, j), p, preferred_element_type=F32)
        m_scr[e] = m_new

    def write_out():
        outs = [acc_scr[e, :V_HEAD_DIM, :] * (1.0 / acc_scr[e, V_HEAD_DIM:V_HEAD_DIM + 1, :]) for e in heads]
        o_ref[0] = jnp.concatenate(outs, axis=0).T.astype(BF16)

    def all_heads(j, masked, finish):
        pending = [scores(e, j, masked) for e in range(SCORES_AHEAD)]
        for e in heads:
            s = pending.pop(0)
            if e + SCORES_AHEAD < HEADS_PER_STEP:
                pending.append(scores(e + SCORES_AHEAD, j, masked))
            finish(e, j, s)

    def run(first, finish):
        load_q()
        all_heads(qi, True, first)

        def body(j, carry):
            all_heads(j, False, finish)
            return carry

        lax.fori_loop(0, qi, body, 0)
        write_out()

    run(first_finish, one_pass_finish)
    worst_sum = functools.reduce(jnp.maximum, [jnp.max(acc_scr[e, V_HEAD_DIM:V_HEAD_DIM + 1, :]) for e in heads])

    @pl.when(worst_sum > SUM_LIMIT)
    def _():
        for e in heads:
            m_scr[e] = jnp.full((1, t), -jnp.inf, F32)
            acc_scr[e] = jnp.zeros((V_HEAD_DIM + L_ROWS, t), F32)
        run(two_pass_finish, two_pass_finish)


def _mla_attn(qt, k, vt):
    b, _, s = qt.shape
    t = ATTN_TILE
    nt = s // t
    groups = MLA_HEADS // HEADS_PER_STEP
    return pl.pallas_call(
        functools.partial(_attn_kernel, t=t),
        grid=(b, groups, nt),
        in_specs=[
            pl.BlockSpec((1, HEADS_PER_STEP * HEAD_SLAB, t), lambda bi, p, qi: (bi, p, qi)),
            pl.BlockSpec((1, s, HEADS_PER_STEP * HEAD_SLAB), lambda bi, p, qi: (bi, 0, p)),
            pl.BlockSpec((1, nt, HEADS_PER_STEP * V_HEAD_DIM, t), lambda bi, p, qi: (bi, 0, p, 0)),
        ],
        out_specs=pl.BlockSpec((1, t, HEADS_PER_STEP * V_HEAD_DIM), lambda bi, p, qi: (bi, qi, p)),
        out_shape=jax.ShapeDtypeStruct((b, s, MLA_HEADS * V_HEAD_DIM), BF16),
        scratch_shapes=[pltpu.VMEM((HEADS_PER_STEP, HEAD_SLAB, t), BF16),
                        pltpu.VMEM((HEADS_PER_STEP, 1, t), F32),
                        pltpu.VMEM((HEADS_PER_STEP, V_HEAD_DIM + L_ROWS, t), F32)],
        compiler_params=pltpu.CompilerParams(
            dimension_semantics=("arbitrary", "arbitrary", "arbitrary"), vmem_limit_bytes=VMEM_LIMIT),
        name="mla_attn",
    )(qt, k, vt)


def _gelu_tanh(x):
    return x * (0.5 * (1.0 + jnp.tanh(0.7978845608028654 * (x + 0.044715 * (x * x * x)))))


def _mixer_out_kernel(x_ref, attn_ref, c2_ref, p_ref, wo_ref, g_mpost_ref, g_fpre_ref, wg_ref, wu_ref,
                      dww_ref, dwb_ref, wd_ref, g_fpost_ref, wpp_ref, g_ple_ref, wpg_ref,
                      o_ref, gcarry_ref, *, tm, sub):
    subs = range(tm // sub)
    rows = [slice(h * sub, (h + 1) * sub) for h in subs]

    @pl.when(pl.program_id(1) == 0)
    def _():
        gcarry_ref[...] = jnp.zeros((FFN_HALO, D_FF), F32)

    mix = [jnp.dot(jnp.concatenate([attn_ref[0, r, :], c2_ref[0, r, :]], axis=-1), wo_ref[...],
                   preferred_element_type=F32) for r in rows]
    ple = [jnp.dot(p_ref[0, r, :].astype(BF16), wpp_ref[...], preferred_element_type=F32) for r in rows]
    h1 = [x_ref[0, rows[h], :] + _rms(mix[h], g_mpost_ref[...]) for h in subs]
    hn = [_rms(h1[h], g_fpre_ref[...]).astype(BF16) for h in subs]

    starts = list(range(0, D_FF, FFN_CHUNK))

    def gate_up(h, c0):
        cols = slice(c0, min(c0 + FFN_CHUNK, D_FF))
        return (jnp.dot(hn[h], wg_ref[:, cols], preferred_element_type=F32),
                jnp.dot(hn[h], wu_ref[:, cols], preferred_element_type=F32))

    def ffn_chunks(h, pending):
        ffn = jnp.zeros((sub, D_MODEL), F32)
        for n, c0 in enumerate(starts):
            cols = slice(c0, min(c0 + FFN_CHUNK, D_FF))
            gate, up = pending.pop(0)
            if n + FFN_AHEAD < len(starts):
                pending.append(gate_up(h, starts[n + FFN_AHEAD]))
            gext = jnp.concatenate([gcarry_ref[:, cols], gate], axis=0)
            g = (dwb_ref[:, cols] + dww_ref[0:1, cols] * gext[FFN_HALO - 2:FFN_HALO - 2 + sub]
                 + dww_ref[1:2, cols] * gext[FFN_HALO - 1:FFN_HALO - 1 + sub] + dww_ref[2:3, cols] * gate)
            gcarry_ref[:, cols] = gate[sub - FFN_HALO:]
            act = (_gelu_tanh(g) * up).astype(BF16)
            ffn = ffn + jnp.dot(act, wd_ref[cols, :], preferred_element_type=F32)
        return ffn

    pending = [gate_up(0, c0) for c0 in starts[:FFN_AHEAD]]
    for h in subs:
        ffn = ffn_chunks(h, pending)
        if h + 1 < len(subs):
            pending = [gate_up(h + 1, c0) for c0 in starts[:FFN_AHEAD]]
        h2 = h1[h] + _rms(ffn, g_fpost_ref[...])
        gt = jax.nn.sigmoid(jnp.dot(h2.astype(BF16), wpg_ref[...], preferred_element_type=F32))
        o_ref[0, rows[h], :] = h2 + gt * _rms(ple[h], g_ple_ref[...])


def _mixer_out(x, attn, c2, p, wo, g_mpost, g_fpre, wg, wu, dww, dwb, wd, g_fpost, wpp, g_ple, wpg):
    b, s, d = x.shape
    tm, sub = TM_OUT, SUB_OUT
    nt = s // tm
    tok = lambda w: pl.BlockSpec((1, tm, w), lambda bi, i: (bi, i, 0))
    return pl.pallas_call(
        functools.partial(_mixer_out_kernel, tm=tm, sub=sub),
        grid=(b, nt),
        in_specs=[
            tok(d), tok(MLA_HEADS * V_HEAD_DIM), tok(CONV_CHANNELS), tok(PLE_DIM),
            _const_spec(wo.shape), _const_spec((1, d)), _const_spec((1, d)),
            _const_spec(wg.shape), _const_spec(wu.shape), _const_spec(dww.shape), _const_spec((1, D_FF)),
            _const_spec(wd.shape), _const_spec((1, d)), _const_spec(wpp.shape), _const_spec((1, d)),
            _const_spec(wpg.shape),
        ],
        out_specs=tok(d),
        out_shape=jax.ShapeDtypeStruct((b, s, d), F32),
        scratch_shapes=[pltpu.VMEM((FFN_HALO, D_FF), F32)],
        compiler_params=pltpu.CompilerParams(
            dimension_semantics=("arbitrary", "arbitrary"), vmem_limit_bytes=VMEM_LIMIT),
        name="mixer_out",
    )(x, attn, c2, p, wo, g_mpost, g_fpre, wg, wu, dww, dwb, wd, g_fpost, wpp, g_ple, wpg)


def kernel(x, p, positions, g_mix_pre, w_in, g_q_a, w_q_b, g_kv_a, w_kv_b, conv_w, conv_b, conv_ln_g, conv_ln_b, w_o, g_mix_post, g_ffn_pre, w_ffn_gate, w_ffn_up, ffn_dw_w, ffn_dw_b, w_ffn_down, g_ffn_post, w_ple_proj, g_ple, w_ple_gate):
    b, s, d = x.shape
    assert w_in.shape[0] == 1 and (b, s, d) == (x.shape[0], x.shape[1], D_MODEL)
    assert s % TM_IN == 0 and s % TM_OUT == 0 and TM_IN % ATTN_TILE == 0 and ATTN_TILE % SUB_IN == 0
    row = lambda v: v[0][None, :].astype(F32)

    w_in0 = w_in[0]
    o_kr = Q_LORA_RANK + KV_LORA_RANK
    o_conv = o_kr + QK_ROPE_DIM
    w_kr = w_in0[:, o_kr:o_conv]
    w_kp = jnp.concatenate([-w_kr[:, HALF_ROPE:], w_kr[:, :HALF_ROPE]], axis=1)
    zc = lambda n: jnp.zeros((d, n), F32)
    slab = lambda w: jnp.concatenate([zc(QK_NOPE_DIM), w, zc(HEAD_SLAB - QK_DIM)], axis=1)
    w1 = jnp.concatenate([w_in0[:, :o_kr], w_in0[:, o_conv:], slab(w_kr), slab(w_kp)], axis=1).astype(BF16)

    wqt = w_q_b[0].T.astype(BF16)
    wkv = w_kv_b[0].astype(BF16)
    wvt = (w_kv_b[0].reshape(KV_LORA_RANK, MLA_HEADS, QK_NOPE_DIM + V_HEAD_DIM)[:, :, QK_NOPE_DIM:]
           .reshape(KV_LORA_RANK, MLA_HEADS * V_HEAD_DIM).T.astype(BF16))
    invf = (ROPE_THETA ** (-jnp.arange(0, QK_ROPE_DIM, 2, dtype=F32) / QK_ROPE_DIM))[:, None]
    pos3 = positions.reshape(b, 1, s)

    qt, k, vt, c2 = _mixer_in(pos3, x, row(g_mix_pre), w1, row(g_q_a), wqt, row(g_kv_a), wkv, wvt, invf,
                              conv_w[0].astype(F32), row(conv_b), row(conv_ln_g), row(conv_ln_b))
    attn = _mla_attn(qt, k, vt)
    return _mixer_out(x, attn, c2, p[0], w_o[0].astype(BF16), row(g_mix_post), row(g_ffn_pre),
                      w_ffn_gate[0].astype(BF16), w_ffn_up[0].astype(BF16), ffn_dw_w[0].astype(F32),
                      row(ffn_dw_b), w_ffn_down[0].astype(BF16), row(g_ffn_post),
                      w_ple_proj[0].astype(BF16), row(g_ple), w_ple_gate[0].astype(BF16))
```

```python
import functools

import jax
import jax.numpy as jnp
from jax import lax
from jax.experimental import pallas as pl
from jax.experimental.pallas import tpu as pltpu

D_MODEL = 1024
PLE_DIM = 256
MLA_HEADS = 8
QK_NOPE_DIM = 64
QK_ROPE_DIM = 32
QK_DIM = QK_NOPE_DIM + QK_ROPE_DIM
V_HEAD_DIM = 64
Q_LORA_RANK = 384
KV_LORA_RANK = 256
ROPE_THETA = 10000.0
CONV_CHANNELS = 512
CONV_WIDTH = 31
D_FF = 2816
FFN_CONV_WIDTH = 3
NORM_EPS = 1e-6

LANES = 128
SUBLANES = 8
FFN_CHUNK = 256
FFN_AHEAD = 3
HEAD_SLAB = LANES
SHIFT_ROWS = 16
HALF_ROPE = QK_ROPE_DIM // 2
CONV_HALO = 32
FFN_HALO = 8
SCORE_SCALE = (QK_DIM ** -0.5) * 1.4426950408889634

TM_IN = 1024
ATTN_TILE = 512
SUB_IN = 256
TM_OUT = 512
SUB_OUT = 256
VMEM_LIMIT = 56 * 1024 * 1024

BF16 = jnp.bfloat16
F32 = jnp.float32
NT_DIMS = (((1,), (1,)), ((), ()))


def _rms(x, g):
    return x * lax.rsqrt(jnp.mean(x * x, axis=-1, keepdims=True) + NORM_EPS) * g


def _const_spec(shape):
    nd = len(shape)
    return pl.BlockSpec(shape, lambda *_: (0,) * nd, pipeline_mode=pl.Buffered(1))


def _conformer_conv(c, h, sub, convw_ref, convb_ref, lng_ref, lnb_ref, cext_ref, shift_ref):
    lo = h * sub
    cext_ref[CONV_HALO + lo:CONV_HALO + lo + sub, :] = c
    span = sub + CONV_HALO - SUBLANES
    for r in range(1, SUBLANES):
        shift_ref[r - 1, h % 2, 0:span, :] = cext_ref[lo + r:lo + r + span, :]
    acc = jnp.broadcast_to(convb_ref[...], (sub, CONV_CHANNELS))
    for j in range(CONV_WIDTH):
        a, r = divmod(CONV_HALO - (CONV_WIDTH - 1) + j, SUBLANES)
        if r == 0:
            tap = cext_ref[lo + a * SUBLANES:lo + a * SUBLANES + sub, :]
        else:
            tap = shift_ref[r - 1, h % 2, a * SUBLANES:a * SUBLANES + sub, :]
        acc = acc + convw_ref[j:j + 1, :] * tap
    mu = jnp.mean(acc, axis=-1, keepdims=True)
    xc = acc - mu
    y = xc * lax.rsqrt(jnp.mean(xc * xc, axis=-1, keepdims=True) + NORM_EPS) * lng_ref[...] + lnb_ref[...]
    return (y * jax.nn.sigmoid(y)).astype(BF16)


def _mixer_in_kernel(pos_ref, x_ref, g_pre_ref, w1_ref, gq_ref, wqt_ref, gkv_ref, wkv_ref, wvt_ref,
                     invf_ref, convw_ref, convb_ref, lng_ref, lnb_ref,
                     qt_ref, k_ref, vt_ref, c2_ref, cext_ref, shift_ref, *, tm, sub):
    o_kv = Q_LORA_RANK
    o_a = o_kv + KV_LORA_RANK
    o_g = o_a + CONV_CHANNELS
    o_kr = o_g + CONV_CHANNELS
    o_kp = o_kr + LANES

    @pl.when(pl.program_id(1) == 0)
    def _():
        cext_ref[0:CONV_HALO, :] = jnp.zeros((CONV_HALO, CONV_CHANNELS), F32)

    for h in range(tm // sub):
        rows = slice(h * sub, (h + 1) * sub)
        xn = _rms(x_ref[0, rows, :], g_pre_ref[...]).astype(BF16)
        proj = jnp.dot(xn, w1_ref[...], preferred_element_type=F32)

        c = proj[:, o_a:o_g] * jax.nn.sigmoid(proj[:, o_g:o_kr])
        c2_ref[0, rows, :] = _conformer_conv(c, h, sub, convw_ref, convb_ref, lng_ref, lnb_ref,
                                             cext_ref, shift_ref)

        qn = _rms(proj[:, :o_kv], gq_ref[...]).astype(BF16)
        kvn = _rms(proj[:, o_kv:o_a], gkv_ref[...]).astype(BF16)

        ang = invf_ref[...] * pos_ref[0, :, rows].astype(F32)
        cos_t = jnp.cos(ang)
        sin_t = jnp.sin(ang)

        qt = lax.dot_general(wqt_ref[...], qn, NT_DIMS, preferred_element_type=F32) * SCORE_SCALE
        zero_rows = jnp.zeros((HEAD_SLAB - QK_DIM, sub), BF16)
        for hd in range(MLA_HEADS):
            r0 = hd * QK_DIM
            o0 = hd * HEAD_SLAB
            x1 = qt[r0 + QK_NOPE_DIM:r0 + QK_NOPE_DIM + HALF_ROPE]
            x2 = qt[r0 + QK_NOPE_DIM + HALF_ROPE:r0 + QK_DIM]
            qt_ref[0, o0:o0 + QK_NOPE_DIM, rows] = qt[r0:r0 + QK_NOPE_DIM].astype(BF16)
            qt_ref[0, o0 + QK_NOPE_DIM:o0 + QK_NOPE_DIM + HALF_ROPE, rows] = (x1 * cos_t - x2 * sin_t).astype(BF16)
            qt_ref[0, o0 + QK_NOPE_DIM + HALF_ROPE:o0 + QK_DIM, rows] = (x1 * sin_t + x2 * cos_t).astype(BF16)
            qt_ref[0, o0 + QK_DIM:o0 + HEAD_SLAB, rows] = zero_rows

        pad_lo = jnp.zeros((QK_NOPE_DIM, sub), F32)
        pad_hi = jnp.zeros((HEAD_SLAB - QK_DIM, sub), F32)
        cos_tok = jnp.concatenate([pad_lo, cos_t, cos_t, pad_hi], axis=0).T
        sin_tok = jnp.concatenate([pad_lo, sin_t, sin_t, pad_hi], axis=0).T
        k_rope = proj[:, o_kr:o_kp] * cos_tok + proj[:, o_kp:o_kp + LANES] * sin_tok

        kfull = jnp.dot(kvn, wkv_ref[...], preferred_element_type=F32)
        lane = lax.broadcasted_iota(jnp.int32, (sub, HEAD_SLAB), 1)
        k_rope = jnp.where((lane >= QK_DIM) & (lane < QK_DIM + SHIFT_ROWS), 1.0, k_rope)
        for hd in range(MLA_HEADS):
            slab = kfull[:, hd * HEAD_SLAB:(hd + 1) * HEAD_SLAB]
            k_ref[0, rows, hd * HEAD_SLAB:(hd + 1) * HEAD_SLAB] = (
                jnp.where(lane < QK_NOPE_DIM, slab, k_rope).astype(BF16))

        vt = lax.dot_general(wvt_ref[...], kvn, NT_DIMS, preferred_element_type=F32)
        per_kv = ATTN_TILE // sub
        vt_ref[0, h // per_kv, :, (h % per_kv) * sub:(h % per_kv + 1) * sub] = vt.astype(BF16)

    cext_ref[0:CONV_HALO, :] = cext_ref[tm:tm + CONV_HALO, :]


def _mixer_in(pos3, x, g_pre, w1, gq, wqt, gkv, wkv, wvt, invf, convw, convb, lng, lnb):
    b, s, d = x.shape
    tm, sub = TM_IN, SUB_IN
    nt = s // tm
    n1 = w1.shape[1]
    chan = _const_spec((1, CONV_CHANNELS))
    return pl.pallas_call(
        functools.partial(_mixer_in_kernel, tm=tm, sub=sub),
        grid=(b, nt),
        in_specs=[
            pl.BlockSpec((1, 1, tm), lambda bi, i: (bi, 0, i)),
            pl.BlockSpec((1, tm, d), lambda bi, i: (bi, i, 0)),
            _const_spec((1, d)),
            _const_spec((d, n1)),
            _const_spec((1, Q_LORA_RANK)),
            _const_spec(wqt.shape),
            _const_spec((1, KV_LORA_RANK)),
            _const_spec(wkv.shape),
            _const_spec(wvt.shape),
            _const_spec(invf.shape),
            _const_spec(convw.shape), chan, chan, chan,
        ],
        out_specs=[
            pl.BlockSpec((1, MLA_HEADS * HEAD_SLAB, tm), lambda bi, i: (bi, 0, i)),
            pl.BlockSpec((1, tm, MLA_HEADS * HEAD_SLAB), lambda bi, i: (bi, i, 0)),
            pl.BlockSpec((1, tm // ATTN_TILE, MLA_HEADS * V_HEAD_DIM, ATTN_TILE), lambda bi, i: (bi, i, 0, 0)),
            pl.BlockSpec((1, tm, CONV_CHANNELS), lambda bi, i: (bi, i, 0)),
        ],
        out_shape=[
            jax.ShapeDtypeStruct((b, MLA_HEADS * HEAD_SLAB, s), BF16),
            jax.ShapeDtypeStruct((b, s, MLA_HEADS * HEAD_SLAB), BF16),
            jax.ShapeDtypeStruct((b, s // ATTN_TILE, MLA_HEADS * V_HEAD_DIM, ATTN_TILE), BF16),
            jax.ShapeDtypeStruct((b, s, CONV_CHANNELS), BF16),
        ],
        scratch_shapes=[pltpu.VMEM((CONV_HALO + tm, CONV_CHANNELS), F32),
                        pltpu.VMEM((SUBLANES - 1, 2, CONV_HALO + sub, CONV_CHANNELS), F32)],
        compiler_params=pltpu.CompilerParams(
            dimension_semantics=("arbitrary", "arbitrary"), vmem_limit_bytes=VMEM_LIMIT,
            allow_input_fusion=[n in (3, 5, 7, 8) for n in range(14)]),
        name="mixer_in",
    )(pos3, x, g_pre, w1, gq, wqt, gkv, wkv, wvt, invf, convw, convb, lng, lnb)


HEADS_PER_STEP = 8
SCORES_AHEAD = 2
L_ROWS = 16
SUM_LIMIT = 2.0 ** 20


def _attn_kernel(qt_ref, k_ref, vt_ref, o_ref, q_scr, m_scr, acc_scr, *, t):
    qi = pl.program_id(2)
    heads = range(HEADS_PER_STEP)
    ones = jnp.ones((L_ROWS, t), BF16)

    def load_q():
        for e in heads:
            q_scr[e] = qt_ref[0, e * HEAD_SLAB:(e + 1) * HEAD_SLAB, :]

    def set_shift(e, m):
        rows = jnp.broadcast_to(m * (-1.0 / SHIFT_ROWS), (SHIFT_ROWS, t))
        q_scr[e, QK_DIM:QK_DIM + SHIFT_ROWS, :] = rows.astype(BF16)

    def bf16_value(x):
        return x.astype(BF16).astype(F32)

    def scores(e, j, masked):
        kj = k_ref[0, pl.ds(pl.multiple_of(j * t, t), t), e * HEAD_SLAB:(e + 1) * HEAD_SLAB]
        s = jnp.dot(kj, q_scr[e], preferred_element_type=F32)
        if masked:
            row = lax.broadcasted_iota(jnp.int32, (t, t), 0)
            col = lax.broadcasted_iota(jnp.int32, (t, t), 1)
            s = jnp.where(row <= col, s, -jnp.inf)
        return s

    def v_aug(e, j):
        return jnp.concatenate([vt_ref[0, j, e * V_HEAD_DIM:(e + 1) * V_HEAD_DIM, :], ones], axis=0)

    def first_finish(e, j, s):
        m = bf16_value(jnp.max(s, axis=0, keepdims=True))
        p = jnp.exp2(s - m).astype(BF16)
        acc_scr[e] = jnp.dot(v_aug(e, j), p, preferred_element_type=F32)
        set_shift(e, m)

    def one_pass_finish(e, j, s):
        acc_scr[e] += jnp.dot(v_aug(e, j), jnp.exp2(s).astype(BF16), preferred_element_type=F32)

    def two_pass_finish(e, j, s):
        m = m_scr[e]
        m_new = jnp.maximum(m, jnp.max(s, axis=0, keepdims=True))
        p = jnp.exp2(s - m_new).astype(BF16)
        acc_scr[e] = jnp.exp2(m - m_new) * acc_scr[e] + jnp.dot(v_aug(e, j), p, preferred_element_type=F32)
        m_scr[e] = m_new

    def write_out():
        outs = [acc_scr[e, :V_HEAD_DIM, :] * (1.0 / acc_scr[e, V_HEAD_DIM:V_HEAD_DIM + 1, :]) for e in heads]
        o_ref[0] = jnp.concatenate(outs, axis=0).T.astype(BF16)

    def all_heads(j, masked, finish):
        pending = [scores(e, j, masked) for e in range(SCORES_AHEAD)]
        for e in heads:
            s = pending.pop(0)
            if e + SCORES_AHEAD < HEADS_PER_STEP:
                pending.append(scores(e + SCORES_AHEAD, j, masked))
            finish(e, j, s)

    def run(first, finish):
        load_q()
        all_heads(qi, True, first)

        def body(j, carry):
            all_heads(j, False, finish)
            return carry

        lax.fori_loop(0, qi, body, 0)
        write_out()

    run(first_finish, one_pass_finish)
    worst_sum = functools.reduce(jnp.maximum, [jnp.max(acc_scr[e, V_HEAD_DIM:V_HEAD_DIM + 1, :]) for e in heads])

    @pl.when(worst_sum > SUM_LIMIT)
    def _():
        for e in heads:
            m_scr[e] = jnp.full((1, t), -jnp.inf, F32)
            acc_scr[e] = jnp.zeros((V_HEAD_DIM + L_ROWS, t), F32)
        run(two_pass_finish, two_pass_finish)


def _mla_attn(qt, k, vt):
    b, _, s = qt.shape
    t = ATTN_TILE
    nt = s // t
    groups = MLA_HEADS // HEADS_PER_STEP
    return pl.pallas_call(
        functools.partial(_attn_kernel, t=t),
        grid=(b, groups, nt),
        in_specs=[
            pl.BlockSpec((1, HEADS_PER_STEP * HEAD_SLAB, t), lambda bi, p, qi: (bi, p, qi)),
            pl.BlockSpec((1, s, HEADS_PER_STEP * HEAD_SLAB), lambda bi, p, qi: (bi, 0, p)),
            pl.BlockSpec((1, nt, HEADS_PER_STEP * V_HEAD_DIM, t), lambda bi, p, qi: (bi, 0, p, 0)),
        ],
        out_specs=pl.BlockSpec((1, t, HEADS_PER_STEP * V_HEAD_DIM), lambda bi, p, qi: (bi, qi, p)),
        out_shape=jax.ShapeDtypeStruct((b, s, MLA_HEADS * V_HEAD_DIM), BF16),
        scratch_shapes=[pltpu.VMEM((HEADS_PER_STEP, HEAD_SLAB, t), BF16),
                        pltpu.VMEM((HEADS_PER_STEP, 1, t), F32),
                        pltpu.VMEM((HEADS_PER_STEP, V_HEAD_DIM + L_ROWS, t), F32)],
        compiler_params=pltpu.CompilerParams(
            dimension_semantics=("arbitrary", "arbitrary", "arbitrary"), vmem_limit_bytes=VMEM_LIMIT),
        name="mla_attn",
    )(qt, k, vt)


def _gelu_tanh(x):
    return x * (0.5 * (1.0 + jnp.tanh(0.7978845608028654 * (x + 0.044715 * (x * x * x)))))


def _mixer_out_kernel(x_ref, attn_ref, c2_ref, p_ref, wo_ref, g_mpost_ref, g_fpre_ref, wg_ref, wu_ref,
                      dww_ref, dwb_ref, wd_ref, g_fpost_ref, wpp_ref, g_ple_ref, wpg_ref,
                      o_ref, gcarry_ref, *, tm, sub):
    subs = range(tm // sub)
    rows = [slice(h * sub, (h + 1) * sub) for h in subs]

    @pl.when(pl.program_id(1) == 0)
    def _():
        gcarry_ref[...] = jnp.zeros((FFN_HALO, D_FF), F32)

    mix = [jnp.dot(jnp.concatenate([attn_ref[0, r, :], c2_ref[0, r, :]], axis=-1), wo_ref[...],
                   preferred_element_type=F32) for r in rows]
    ple = [jnp.dot(p_ref[0, r, :].astype(BF16), wpp_ref[...], preferred_element_type=F32) for r in rows]
    h1 = [x_ref[0, rows[h], :] + _rms(mix[h], g_mpost_ref[...]) for h in subs]
    hn = [_rms(h1[h], g_fpre_ref[...]).astype(BF16) for h in subs]

    starts = list(range(0, D_FF, FFN_CHUNK))

    def gate_up(h, c0):
        cols = slice(c0, min(c0 + FFN_CHUNK, D_FF))
        return (jnp.dot(hn[h], wg_ref[:, cols], preferred_element_type=F32),
                jnp.dot(hn[h], wu_ref[:, cols], preferred_element_type=F32))

    def ffn_chunks(h, pending):
        ffn = jnp.zeros((sub, D_MODEL), F32)
        for n, c0 in enumerate(starts):
            cols = slice(c0, min(c0 + FFN_CHUNK, D_FF))
            gate, up = pending.pop(0)
            if n + FFN_AHEAD < len(starts):
                pending.append(gate_up(h, starts[n + FFN_AHEAD]))
            gext = jnp.concatenate([gcarry_ref[:, cols], gate], axis=0)
            g = (dwb_ref[:, cols] + dww_ref[0:1, cols] * gext[FFN_HALO - 2:FFN_HALO - 2 + sub]
                 + dww_ref[1:2, cols] * gext[FFN_HALO - 1:FFN_HALO - 1 + sub] + dww_ref[2:3, cols] * gate)
            gcarry_ref[:, cols] = gate[sub - FFN_HALO:]
            act = (_gelu_tanh(g) * up).astype(BF16)
            ffn = ffn + jnp.dot(act, wd_ref[cols, :], preferred_element_type=F32)
        return ffn

    pending = [gate_up(0, c0) for c0 in starts[:FFN_AHEAD]]
    for h in subs:
        ffn = ffn_chunks(h, pending)
        if h + 1 < len(subs):
            pending = [gate_up(h + 1, c0) for c0 in starts[:FFN_AHEAD]]
        h2 = h1[h] + _rms(ffn, g_fpost_ref[...])
        gt = jax.nn.sigmoid(jnp.dot(h2.astype(BF16), wpg_ref[...], preferred_element_type=F32))
        o_ref[0, rows[h], :] = h2 + gt * _rms(ple[h], g_ple_ref[...])


def _mixer_out(x, attn, c2, p, wo, g_mpost, g_fpre, wg, wu, dww, dwb, wd, g_fpost, wpp, g_ple, wpg):
    b, s, d = x.shape
    tm, sub = TM_OUT, SUB_OUT
    nt = s // tm
    tok = lambda w: pl.BlockSpec((1, tm, w), lambda bi, i: (bi, i, 0))
    return pl.pallas_call(
        functools.partial(_mixer_out_kernel, tm=tm, sub=sub),
        grid=(b, nt),
        in_specs=[
            tok(d), tok(MLA_HEADS * V_HEAD_DIM), tok(CONV_CHANNELS), tok(PLE_DIM),
            _const_spec(wo.shape), _const_spec((1, d)), _const_spec((1, d)),
            _const_spec(wg.shape), _const_spec(wu.shape), _const_spec(dww.shape), _const_spec((1, D_FF)),
            _const_spec(wd.shape), _const_spec((1, d)), _const_spec(wpp.shape), _const_spec((1, d)),
            _const_spec(wpg.shape),
        ],
        out_specs=tok(d),
        out_shape=jax.ShapeDtypeStruct((b, s, d), F32),
        scratch_shapes=[pltpu.VMEM((FFN_HALO, D_FF), F32)],
        compiler_params=pltpu.CompilerParams(
            dimension_semantics=("arbitrary", "arbitrary"), vmem_limit_bytes=VMEM_LIMIT,
            allow_input_fusion=[n in (4, 7, 8, 11, 13, 15) for n in range(16)]),
        name="mixer_out",
    )(x, attn, c2, p, wo, g_mpost, g_fpre, wg, wu, dww, dwb, wd, g_fpost, wpp, g_ple, wpg)


def kernel(x, p, positions, g_mix_pre, w_in, g_q_a, w_q_b, g_kv_a, w_kv_b, conv_w, conv_b, conv_ln_g, conv_ln_b, w_o, g_mix_post, g_ffn_pre, w_ffn_gate, w_ffn_up, ffn_dw_w, ffn_dw_b, w_ffn_down, g_ffn_post, w_ple_proj, g_ple, w_ple_gate):
    b, s, d = x.shape
    assert w_in.shape[0] == 1 and (b, s, d) == (x.shape[0], x.shape[1], D_MODEL)
    assert s % TM_IN == 0 and s % TM_OUT == 0 and TM_IN % ATTN_TILE == 0 and ATTN_TILE % SUB_IN == 0
    row = lambda v: v[0][None, :].astype(F32)

    w_in0 = w_in[0]
    o_kr = Q_LORA_RANK + KV_LORA_RANK
    o_conv = o_kr + QK_ROPE_DIM
    w_kr = w_in0[:, o_kr:o_conv]
    w_kp = jnp.concatenate([-w_kr[:, HALF_ROPE:], w_kr[:, :HALF_ROPE]], axis=1)
    zc = lambda n: jnp.zeros((d, n), F32)
    slab = lambda w: jnp.concatenate([zc(QK_NOPE_DIM), w, zc(HEAD_SLAB - QK_DIM)], axis=1)
    w1 = jnp.concatenate([w_in0[:, :o_kr], w_in0[:, o_conv:], slab(w_kr), slab(w_kp)], axis=1).astype(BF16)

    wqt = w_q_b[0].T.astype(BF16)
    wkv = w_kv_b[0].astype(BF16)
    wvt = (w_kv_b[0].reshape(KV_LORA_RANK, MLA_HEADS, QK_NOPE_DIM + V_HEAD_DIM)[:, :, QK_NOPE_DIM:]
           .reshape(KV_LORA_RANK, MLA_HEADS * V_HEAD_DIM).T.astype(BF16))
    invf = (ROPE_THETA ** (-jnp.arange(0, QK_ROPE_DIM, 2, dtype=F32) / QK_ROPE_DIM))[:, None]
    pos3 = positions.reshape(b, 1, s)

    qt, k, vt, c2 = _mixer_in(pos3, x, row(g_mix_pre), w1, row(g_q_a), wqt, row(g_kv_a), wkv, wvt, invf,
                              conv_w[0].astype(F32), row(conv_b), row(conv_ln_g), row(conv_ln_b))
    attn = _mla_attn(qt, k, vt)
    return _mixer_out(x, attn, c2, p[0], w_o[0].astype(BF16), row(g_mix_post), row(g_ffn_pre),
                      w_ffn_gate[0].astype(BF16), w_ffn_up[0].astype(BF16), ffn_dw_w[0].astype(F32),
                      row(ffn_dw_b), w_ffn_down[0].astype(BF16), row(g_ffn_post),
                      w_ple_proj[0].astype(BF16), row(g_ple), w_ple_gate[0].astype(BF16))
```
